```python
import math, functools
import jax, jax.numpy as jnp
from jax import lax
import numpy as np

D_MODEL = 2048
BATCH = 16
SEQ = 2048
DEPTH = 1
DEC_BATCH = 32
DEC_SEQ = 32
PAST_LEN = 2048

CHUNK = 64
GLA_HEADS = 4
GLA_QK = D_MODEL // 2
GLA_V = D_MODEL
GLA_DK = GLA_QK // GLA_HEADS
GLA_DV = GLA_V // GLA_HEADS
GLA_GATE_RANK = 16
GLA_TAU = 16.0
ATT_HEADS = 8
ATT_DH = 128
ATT_WIDTH = ATT_HEADS * ATT_DH
BAND_CHUNKS = 8
BAND_PAST = BAND_CHUNKS * CHUNK
REL_CLIP = 128
D_FF = ((8 * D_MODEL // 3 + 127) // 128) * 128
CONV_W = 3
DN_ALPHA = (2.0 * DEPTH) ** 0.25
DN_BETA = (8.0 * DEPTH) ** -0.25
LN_EPS = 1e-5
RMS_EPS = 1e-6
NEG_INF = -1e30
IN_SPLITS = (GLA_QK, GLA_QK, GLA_V, GLA_V, GLA_GATE_RANK, ATT_WIDTH, ATT_WIDTH, ATT_WIDTH, D_MODEL, D_MODEL)
IN_WIDTH = 2 * GLA_QK + 2 * GLA_V + GLA_GATE_RANK + 3 * ATT_WIDTH + 2 * D_MODEL

kernel_name = "hybrid_gla_chunkband_convffn_deepnorm_step"


def _split_in(z):
    offs = np.cumsum(np.array(IN_SPLITS))[:-1].tolist()
    return jnp.split(z, offs, axis=-1)


def _layer_norm(x, g, b):
    xf = x.astype(jnp.float32)
    mu = jnp.mean(xf, axis=-1, keepdims=True)
    var = jnp.mean(jnp.square(xf - mu), axis=-1, keepdims=True)
    return ((xf - mu) * lax.rsqrt(var + LN_EPS) * g + b).astype(x.dtype)


def _rel_bias(table, rel):
    return table[:, jnp.clip(rel, -REL_CLIP, REL_CLIP) + REL_CLIP].astype(jnp.float32)


def _gla_recurrence(q, k, v, log_a, s0):
    B, L, H, _ = q.shape
    blk = min(CHUNK, L)
    n = L // blk

    def to_blocks(t):
        return t.reshape(B, n, blk, H, t.shape[-1]).transpose(1, 0, 3, 2, 4)

    qb, kb, vb, ab = to_blocks(q), to_blocks(k), to_blocks(v), to_blocks(log_a)
    causal = jnp.tril(jnp.ones((blk, blk), dtype=bool))

    def step(s, inp):
        qc, kc, vc, ac = inp
        b = jnp.cumsum(ac, axis=-2)
        b_last = b[..., -1:, :]
        q_t = qc * jnp.exp(b)
        k_t = kc * jnp.exp(-b)
        att = jnp.where(causal, jnp.einsum('bhtk,bhsk->bhts', q_t, k_t), 0.0)
        o = jnp.einsum('bhts,bhsv->bhtv', att, vc) + jnp.einsum('bhtk,bhkv->bhtv', q_t, s)
        k_dec = kc * jnp.exp(b_last - b)
        s_new = jnp.exp(b_last[..., 0, :])[..., None] * s + jnp.einsum('bhsk,bhsv->bhkv', k_dec, vc)
        return s_new, o

    s_fin, ob = lax.scan(step, s0, (qb, kb, vb, ab))
    o = ob.transpose(1, 0, 3, 2, 4).reshape(B, L, H, v.shape[-1])
    return o, s_fin


def _band_attention_chunked(q, k, v, rel_table):
    B, S, H, D = q.shape
    nc = S // CHUNK
    nb = BAND_CHUNKS + 1
    shp = (B, nc, CHUNK, H, D)
    pad = jnp.zeros((B, BAND_CHUNKS, CHUNK, H, D), k.dtype)
    kc = jnp.concatenate([pad, k.reshape(shp)], axis=1)
    vc = jnp.concatenate([pad, v.reshape(shp)], axis=1)
    qc = q.reshape(shp)
    scores = jnp.concatenate(
        [jnp.einsum('bnqhd,bnkhd->bnhqk', qc, kc[:, o:o + nc], preferred_element_type=jnp.float32)
         for o in range(nb)], axis=-1)
    koff = jnp.arange(nb * CHUNK) - BAND_CHUNKS * CHUNK
    rel = jnp.arange(CHUNK)[:, None] - koff[None, :]
    scores = scores * (D ** -0.5) + _rel_bias(rel_table, rel)[None, None]
    valid = (jnp.arange(nc)[:, None] * CHUNK + koff[None, :]) >= 0
    scores = jnp.where(valid[None, :, None, None, :], scores, NEG_INF)
    p = jax.nn.softmax(scores, axis=-1).astype(v.dtype)
    out = jnp.einsum('bnhqk,bnkhd->bnqhd', p[..., :CHUNK], vc[:, 0:nc])
    for o in range(1, nb):
        out = out + jnp.einsum('bnhqk,bnkhd->bnqhd', p[..., o * CHUNK:(o + 1) * CHUNK], vc[:, o:o + nc])
    return out.reshape(B, S, H, D)


def _band_attention_step(q, k_all, v_all, rel_table):
    L = q.shape[1]
    n_past = k_all.shape[1] - L
    s = jnp.einsum('bqhd,bkhd->bhqk', q, k_all, preferred_element_type=jnp.float32) * (q.shape[-1] ** -0.5)
    rel = (n_past + jnp.arange(L))[:, None] - jnp.arange(n_past + L)[None, :]
    s = s + _rel_bias(rel_table, rel)[None]
    p = jax.nn.softmax(s, axis=-1).astype(v_all.dtype)
    return jnp.einsum('bhqk,bkhd->bqhd', p, v_all)


def _layer(x, past_k, past_v, s0, conv_prev, prm):
    (w_in, gla_gate_up, gla_gate_b, gla_norm_g, att_rel_bias, merge_b, w_br_gla, w_br_att, w_out,
     ln1_g, ln1_b, w_ffn_up, ffn_conv_w, ffn_conv_b, w_ffn_down, ln2_g, ln2_b) = prm
    f32 = jnp.float32
    B, L, _ = x.shape
    z = x @ w_in
    q_g, k_g, v_g, r_g, a_lo, q_a, k_a, v_a, m_g, m_a = _split_in(z)

    log_a = jax.nn.log_sigmoid((a_lo @ gla_gate_up + gla_gate_b).astype(f32)) / GLA_TAU
    hq = lambda t, d: t.reshape(B, L, GLA_HEADS, d).astype(f32)
    o_gla, s_fin = _gla_recurrence(hq(q_g, GLA_DK) * (GLA_DK ** -0.5), hq(k_g, GLA_DK), hq(v_g, GLA_DV),
                                   hq(log_a, GLA_DK), s0.astype(f32))
    o_gla = o_gla * lax.rsqrt(jnp.mean(jnp.square(o_gla), axis=-1, keepdims=True) + RMS_EPS)
    o_gla = (o_gla.reshape(B, L, GLA_V) * gla_norm_g).astype(x.dtype) * jax.nn.silu(r_g)

    ha = lambda t: t.reshape(B, L, ATT_HEADS, ATT_DH)
    q_a, k_a, v_a = ha(q_a), ha(k_a), ha(v_a)
    if past_k is None:
        o_att = _band_attention_chunked(q_a, k_a, v_a, att_rel_bias)
        keep = min(BAND_PAST, L)
        new_k, new_v = k_a[:, L - keep:], v_a[:, L - keep:]
    else:
        k_all = jnp.concatenate([past_k.astype(k_a.dtype), k_a], axis=1)
        v_all = jnp.concatenate([past_v.astype(v_a.dtype), v_a], axis=1)
        o_att = _band_attention_step(q_a, k_all, v_all, att_rel_bias)
        new_k, new_v = k_a, v_a
    o_att = o_att.reshape(B, L, ATT_WIDTH).astype(x.dtype)

    mix = (jax.nn.sigmoid(m_g + merge_b[0]) * (o_gla @ w_br_gla)
           + jax.nn.sigmoid(m_a + merge_b[1]) * (o_att @ w_br_att)) @ w_out
    h = _layer_norm(DN_ALPHA * x + mix, ln1_g, ln1_b)

    ug = h @ w_ffn_up
    u, g = ug[..., :D_FF], ug[..., D_FF:]
    u_ext = jnp.concatenate([conv_prev.astype(u.dtype), u], axis=1)
    uc = ffn_conv_b + u_ext[:, 0:L] * ffn_conv_w[0]
    for i in range(1, CONV_W):
        uc = uc + u_ext[:, i:i + L] * ffn_conv_w[i]
    ffn = (jax.nn.gelu(uc) * g) @ w_ffn_down
    y = _layer_norm(DN_ALPHA * h + ffn, ln2_g, ln2_b)
    return y, new_k, new_v, s_fin.astype(x.dtype), u_ext[:, -(CONV_W - 1):]


def setup_inputs(seed: int = 0) -> dict:
    key = jax.random.key(seed)
    ks = jax.random.split(key, 24)
    nrm = lambda k, shape, s: jax.random.normal(k, shape, jnp.float32) * s
    n_cache = min(BAND_PAST, PAST_LEN)
    return {
        "x_prompt": nrm(ks[0], (BATCH, SEQ, D_MODEL), 1.0),
        "x_sample": nrm(ks[1], (DEC_BATCH, DEC_SEQ, D_MODEL), 1.0),
        "cache_att_k": nrm(ks[2], (DEPTH, DEC_BATCH, n_cache, ATT_HEADS, ATT_DH), 1.0),
        "cache_att_v": nrm(ks[3], (DEPTH, DEC_BATCH, n_cache, ATT_HEADS, ATT_DH), 1.0),
        "state_gla": nrm(ks[4], (DEPTH, DEC_BATCH, GLA_HEADS, GLA_DK, GLA_DV), 0.5),
        "state_ffn_conv": nrm(ks[5], (DEPTH, DEC_BATCH, CONV_W - 1, D_FF), 1.0),
        "w_in": nrm(ks[6], (DEPTH, D_MODEL, IN_WIDTH), D_MODEL ** -0.5),
        "gla_gate_up": nrm(ks[7], (DEPTH, GLA_GATE_RANK, GLA_QK), GLA_GATE_RANK ** -0.5),
        "gla_gate_b": nrm(ks[8], (DEPTH, GLA_QK), 0.1),
        "gla_norm_g": 1.0 + nrm(ks[9], (DEPTH, GLA_V), 0.05),
        "att_rel_bias": nrm(ks[10], (DEPTH, ATT_HEADS, 2 * REL_CLIP + 1), 0.1),
        "merge_b": nrm(ks[11], (DEPTH, 2, D_MODEL), 0.1),
        "w_br_gla": nrm(ks[12], (DEPTH, GLA_V, D_MODEL), GLA_V ** -0.5),
        "w_br_att": nrm(ks[13], (DEPTH, ATT_WIDTH, D_MODEL), ATT_WIDTH ** -0.5),
        "w_out": nrm(ks[14], (DEPTH, D_MODEL, D_MODEL), DN_BETA * D_MODEL ** -0.5),
        "ln1_g": 1.0 + nrm(ks[15], (DEPTH, D_MODEL), 0.05),
        "ln1_b": nrm(ks[16], (DEPTH, D_MODEL), 0.02),
        "w_ffn_up": nrm(ks[17], (DEPTH, D_MODEL, 2 * D_FF), D_MODEL ** -0.5),
        "ffn_conv_w": nrm(ks[18], (DEPTH, CONV_W, D_FF), CONV_W ** -0.5),
        "ffn_conv_b": nrm(ks[19], (DEPTH, D_FF), 0.02),
        "w_ffn_down": nrm(ks[20], (DEPTH, D_FF, D_MODEL), DN_BETA * D_FF ** -0.5),
        "ln2_g": 1.0 + nrm(ks[21], (DEPTH, D_MODEL), 0.05),
        "ln2_b": nrm(ks[22], (DEPTH, D_MODEL), 0.02),
    }


def reference(x_prompt, x_sample, cache_att_k, cache_att_v, state_gla, state_ffn_conv,
              w_in, gla_gate_up, gla_gate_b, gla_norm_g, att_rel_bias, merge_b,
              w_br_gla, w_br_att, w_out, ln1_g, ln1_b, w_ffn_up, ffn_conv_w, ffn_conv_b,
              w_ffn_down, ln2_g, ln2_b):
    xp, xs = x_prompt, x_sample
    bp = xp.shape[0]
    kp, vp, sp, cp = [], [], [], []
    ksm, vsm, ssm, csm = [], [], [], []
    for l in range(DEPTH):
        prm = (w_in[l], gla_gate_up[l], gla_gate_b[l], gla_norm_g[l], att_rel_bias[l], merge_b[l],
               w_br_gla[l], w_br_att[l], w_out[l], ln1_g[l], ln1_b[l], w_ffn_up[l], ffn_conv_w[l],
               ffn_conv_b[l], w_ffn_down[l], ln2_g[l], ln2_b[l])
        s0_p = jnp.zeros((bp, GLA_HEADS, GLA_DK, GLA_DV), jnp.float32)
        conv0_p = jnp.zeros((bp, CONV_W - 1, D_FF), xp.dtype)
        xp, k1, v1, s1, c1 = _layer(xp, None, None, s0_p, conv0_p, prm)
        kp.append(k1); vp.append(v1); sp.append(s1); cp.append(c1)
        xs, k2, v2, s2, c2 = _layer(xs, cache_att_k[l], cache_att_v[l], state_gla[l], state_ffn_conv[l], prm)
        ksm.append(k2); vsm.append(v2); ssm.append(s2); csm.append(c2)
    return (xp, xs,
            jnp.stack(kp), jnp.stack(vp), jnp.stack(sp), jnp.stack(cp),
            jnp.stack(ksm), jnp.stack(vsm), jnp.stack(ssm), jnp.stack(csm))
```

```python
import functools
import math

import jax
import jax.numpy as jnp
from jax import lax
from jax.experimental import pallas as pl
from jax.experimental.pallas import tpu as pltpu

F32 = jnp.float32
BF16 = jnp.bfloat16

D_MODEL = 2048
CHUNK = 64
CHUNK_SHIFT = CHUNK.bit_length() - 1
assert 1 << CHUNK_SHIFT == CHUNK
GLA_HEADS = 4
GLA_QK = D_MODEL // 2
GLA_V = D_MODEL
GLA_DK = GLA_QK // GLA_HEADS
GLA_DV = GLA_V // GLA_HEADS
GLA_GATE_RANK = 16
GLA_TAU = 16.0
ATT_HEADS = 8
ATT_DH = 128
ATT_WIDTH = ATT_HEADS * ATT_DH
BAND_CHUNKS = 8
BAND_PAST = BAND_CHUNKS * CHUNK
REL_CLIP = 128
D_FF = 5504
CONV_W = 3
LN_EPS = 1e-5
RMS_EPS = 1e-6
NEG_INF = -1e30

V7X_LANES = 128
V7X_VMEM_BYTES = 64 * 1024 * 1024
VMEM_LIMIT = V7X_VMEM_BYTES - 8 * 1024 * 1024

Z_QG = 0
Z_KG = Z_QG + GLA_QK
Z_VG = Z_KG + GLA_QK
Z_RG = Z_VG + GLA_V
Z_MG = Z_RG + GLA_V
Z_MA = Z_MG + D_MODEL
Z_QA = Z_MA + D_MODEL
Z_KA = Z_QA + ATT_WIDTH
Z_VA = Z_KA + ATT_WIDTH
Z_WIDTH = Z_VA + ATT_WIDTH
ALO_PAD = V7X_LANES

INPROJ_BM = 1024
INPROJ_BN = 1024
MERGE_BM = 256
FFN_BM = 512
FFN_BF = 512
D_FF_PAD = ((D_FF + FFN_BF - 1) // FFN_BF) * FFN_BF
ATT_GROUP = 4 * CHUNK
ATT_WINDOW = ATT_GROUP + BAND_PAST


def _cparams(n_axes):
    return pltpu.CompilerParams(
        dimension_semantics=("arbitrary",) * n_axes, vmem_limit_bytes=VMEM_LIMIT)


def _dot(a, b):
    return jnp.dot(a, b, preferred_element_type=F32)


def _dot_nt(a, b):
    return lax.dot_general(a, b, (((1,), (1,)), ((), ())), preferred_element_type=F32)


def _dot_tn(a, b):
    return lax.dot_general(a, b, (((0,), (0,)), ((), ())), preferred_element_type=F32)


def _sigmoid(x):
    return 1.0 / (1.0 + jnp.exp(-x))


def _layer_norm(x, g, b):
    mu = jnp.mean(x, axis=-1, keepdims=True)
    xc = x - mu
    var = jnp.mean(xc * xc, axis=-1, keepdims=True)
    return xc * lax.rsqrt(var + LN_EPS) * g + b


def _inproj_kernel(x_ref, w_ref, walo_ref, z_ref, alo_ref, xb_ref):
    @pl.when(pl.program_id(1) == 0)
    def _():
        xb_ref[...] = x_ref[...].astype(BF16)
        alo_ref[...] = _dot(xb_ref[...], walo_ref[...])

    z_ref[...] = _dot(xb_ref[...], w_ref[...]).astype(BF16)


def _inproj(x2d, w_main, w_alo):
    m = x2d.shape[0]
    bm = min(INPROJ_BM, m)
    grid = (m // bm, Z_WIDTH // INPROJ_BN)
    return pl.pallas_call(
        _inproj_kernel,
        grid=grid,
        in_specs=[
            pl.BlockSpec((bm, D_MODEL), lambda i, j: (i, 0)),
            pl.BlockSpec((D_MODEL, INPROJ_BN), lambda i, j: (0, j)),
            pl.BlockSpec((D_MODEL, ALO_PAD), lambda i, j: (0, 0)),
        ],
        out_specs=[
            pl.BlockSpec((bm, INPROJ_BN), lambda i, j: (i, j)),
            pl.BlockSpec((bm, ALO_PAD), lambda i, j: (i, 0)),
        ],
        out_shape=[
            jax.ShapeDtypeStruct((m, Z_WIDTH), BF16),
            jax.ShapeDtypeStruct((m, ALO_PAD), F32),
        ],
        scratch_shapes=[pltpu.VMEM((bm, D_MODEL), BF16)],
        compiler_params=_cparams(2),
        name="inproj",
    )(x2d, w_main, w_alo)


def _gla_kernel(q_ref, k_ref, v_ref, r_ref, alo_ref, gup_ref, gb_ref, ng_ref, s0_ref,
                o_ref, sfin_ref, st_ref, *, seq_len, blk):
    st_ref[...] = s0_ref[0, 0].T
    row = lax.broadcasted_iota(jnp.int32, (blk, blk), 0)
    col = lax.broadcasted_iota(jnp.int32, (blk, blk), 1)
    causal = row >= col
    tril = jnp.where(causal, 1.0, 0.0).astype(BF16)

    def body(i, carry):
        r0 = pl.multiple_of(i * blk, blk)
        rows = pl.ds(r0, blk)
        x = _dot(alo_ref[rows, :].astype(BF16), gup_ref[...]) + gb_ref[...]
        log_a = -(jnp.maximum(-x, 0.0) + jnp.log(1.0 + jnp.exp(-jnp.abs(x)))) * (1.0 / GLA_TAU)
        hi = log_a.astype(BF16)
        rem = log_a - hi.astype(F32)
        mid = rem.astype(BF16)
        lo = (rem - mid.astype(F32)).astype(BF16)
        b = _dot(tril, hi) + _dot(tril, mid) + _dot(tril, lo)
        b_last = b[blk - 1:blk, :]
        qf = q_ref[rows, :].astype(F32) * (GLA_DK ** -0.5)
        kf = k_ref[rows, :].astype(F32)
        q_t = (qf * jnp.exp(b)).astype(BF16)
        k_t = (kf * jnp.exp(-b)).astype(BF16)
        k_dec = (kf * jnp.exp(b_last - b)).astype(BF16)
        att = jnp.where(causal, _dot_nt(q_t, k_t), 0.0).astype(BF16)
        vb = v_ref[rows, :]
        st = st_ref[...]
        o = _dot(att, vb) + _dot_nt(q_t, st.astype(BF16))
        st_ref[...] = st * jnp.exp(b_last) + _dot_tn(vb, k_dec)
        ms = jnp.mean(o * o, axis=-1, keepdims=True)
        on = o * lax.rsqrt(ms + RMS_EPS) * ng_ref[...]
        rf = r_ref[rows, :].astype(F32)
        o_ref[rows, :] = (on * (rf * _sigmoid(rf))).astype(BF16)
        return carry

    lax.fori_loop(0, seq_len // blk, body, 0)
    sfin_ref[0, 0] = st_ref[...].T


def _gla(z, alo, gup_pad, gate_b, norm_g, s0, *, n_seq, seq_len):
    blk = min(CHUNK, seq_len)
    qb, vb = Z_QG // GLA_DK, Z_VG // GLA_DV
    kb, rb = Z_KG // GLA_DK, Z_RG // GLA_DV
    kern = functools.partial(_gla_kernel, seq_len=seq_len, blk=blk)
    return pl.pallas_call(
        kern,
        grid=(n_seq, GLA_HEADS),
        in_specs=[
            pl.BlockSpec((seq_len, GLA_DK), lambda b, h: (b, qb + h)),
            pl.BlockSpec((seq_len, GLA_DK), lambda b, h: (b, kb + h)),
            pl.BlockSpec((seq_len, GLA_DV), lambda b, h: (b, vb + h)),
            pl.BlockSpec((seq_len, GLA_DV), lambda b, h: (b, rb + h)),
            pl.BlockSpec((seq_len, ALO_PAD), lambda b, h: (b, 0)),
            pl.BlockSpec((ALO_PAD, GLA_DK), lambda b, h: (0, h)),
            pl.BlockSpec((1, GLA_DK), lambda b, h: (0, h)),
            pl.BlockSpec((1, GLA_DV), lambda b, h: (0, h)),
            pl.BlockSpec((1, 1, GLA_DK, GLA_DV), lambda b, h: (b, h, 0, 0)),
        ],
        out_specs=[
            pl.BlockSpec((seq_len, GLA_DV), lambda b, h: (b, h)),
            pl.BlockSpec((1, 1, GLA_DK, GLA_DV), lambda b, h: (b, h, 0, 0)),
        ],
        out_shape=[
            jax.ShapeDtypeStruct((n_seq * seq_len, GLA_V), BF16),
            jax.ShapeDtypeStruct((n_seq, GLA_HEADS, GLA_DK, GLA_DV), F32),
        ],
        scratch_shapes=[pltpu.VMEM((GLA_DV, GLA_DK), F32)],
        compiler_params=_cparams(2),
        name="gla",
    )(z, z, z, z, alo, gup_pad, gate_b, norm_g, s0)


def _rel_bias_kernel(tab_ref, out_ref, *, n_q, n_k, off, band):
    h = pl.program_id(0)
    k8 = lax.broadcasted_iota(jnp.int32, (8, n_k), 1)
    rel0 = jnp.clip(off - k8, -REL_CLIP, REL_CLIP) + REL_CLIP

    def body(r, acc):
        return jnp.where(rel0 == r, tab_ref[h, r], acc)

    row0 = lax.fori_loop(0, 2 * REL_CLIP + 1, body, jnp.zeros((8, n_k), F32))
    full = jnp.broadcast_to(row0[0:1, :], (n_q, n_k))
    rolled = pltpu.roll(full, 0, 1, stride=1, stride_axis=0)
    q = lax.broadcasted_iota(jnp.int32, (n_q, n_k), 0)
    k = lax.broadcasted_iota(jnp.int32, (n_q, n_k), 1)
    bias = jnp.where(k < q, tab_ref[h, 2 * REL_CLIP], rolled)
    if band:
        dc = (k >> CHUNK_SHIFT) - (q >> CHUNK_SHIFT)
        bias = jnp.where(dc < 0, NEG_INF, jnp.where(dc > BAND_CHUNKS, NEG_INF, bias))
    out_ref[0] = bias


def _rel_bias(table, *, n_q, n_k, off, band):
    assert off >= REL_CLIP and n_k % V7X_LANES == 0
    kern = functools.partial(_rel_bias_kernel, n_q=n_q, n_k=n_k, off=off, band=band)
    return pl.pallas_call(
        kern,
        grid=(ATT_HEADS,),
        in_specs=[pl.BlockSpec(memory_space=pltpu.SMEM)],
        out_specs=pl.BlockSpec((1, n_q, n_k), lambda h: (h, 0, 0)),
        out_shape=jax.ShapeDtypeStruct((ATT_HEADS, n_q, n_k), F32),
        compiler_params=_cparams(1),
        name="rel_bias",
    )(table)


def _band_kernel(q_ref, k_ref, v_ref, bias_ref, o_ref, kp_ref, vp_ref, *, seq_len):
    zeros = jnp.zeros((BAND_PAST, ATT_DH), BF16)
    kp_ref[0:BAND_PAST, :] = zeros
    vp_ref[0:BAND_PAST, :] = zeros
    kp_ref[BAND_PAST:, :] = k_ref[...]
    vp_ref[BAND_PAST:, :] = v_ref[...]
    kk = lax.broadcasted_iota(jnp.int32, (ATT_GROUP, ATT_WINDOW), 1)

    def body(g, carry):
        q0 = pl.multiple_of(g * ATT_GROUP, ATT_GROUP)
        qg = q_ref[pl.ds(q0, ATT_GROUP), :]
        kw = kp_ref[pl.ds(q0, ATT_WINDOW), :]
        vw = vp_ref[pl.ds(q0, ATT_WINDOW), :]
        s = _dot_nt(qg, kw) * (ATT_DH ** -0.5) + bias_ref[0]
        s = jnp.where(kk >= BAND_PAST - q0, s, NEG_INF)
        m = jnp.max(s, axis=-1, keepdims=True)
        p = jnp.exp(s - m)
        l = jnp.sum(p, axis=-1, keepdims=True)
        o = _dot(p.astype(BF16), vw) / l
        o_ref[pl.ds(q0, ATT_GROUP), :] = o.astype(BF16)
        return carry

    lax.fori_loop(0, seq_len // ATT_GROUP, body, 0)


def _band_attention(z, bias, *, n_seq, seq_len):
    qb, kb, vb = Z_QA // ATT_DH, Z_KA // ATT_DH, Z_VA // ATT_DH
    kern = functools.partial(_band_kernel, seq_len=seq_len)
    return pl.pallas_call(
        kern,
        grid=(ATT_HEADS, n_seq),
        in_specs=[
            pl.BlockSpec((seq_len, ATT_DH), lambda h, b: (b, qb + h)),
            pl.BlockSpec((seq_len, ATT_DH), lambda h, b: (b, kb + h)),
            pl.BlockSpec((seq_len, ATT_DH), lambda h, b: (b, vb + h)),
            pl.BlockSpec((1, ATT_GROUP, ATT_WINDOW), lambda h, b: (h, 0, 0)),
        ],
        out_specs=pl.BlockSpec((seq_len, ATT_DH), lambda h, b: (b, h)),
        out_shape=jax.ShapeDtypeStruct((n_seq * seq_len, ATT_WIDTH), BF16),
        scratch_shapes=[pltpu.VMEM((seq_len + BAND_PAST, ATT_DH), BF16),
                        pltpu.VMEM((seq_len + BAND_PAST, ATT_DH), BF16)],
        compiler_params=_cparams(2),
        name="band_attention",
    )(z, z, z, bias)


def _band_step_kernel(q_ref, kn_ref, vn_ref, kc_ref, vc_ref, bias_ref, o_ref, *, n_new, n_past):
    scale = ATT_DH ** -0.5
    for h in range(ATT_HEADS):
        lanes = slice(h * ATT_DH, (h + 1) * ATT_DH)
        qh = q_ref[:, lanes]
        kc = kc_ref[0, pl.ds(h, n_past, stride=ATT_HEADS), :].astype(BF16)
        vc = vc_ref[0, pl.ds(h, n_past, stride=ATT_HEADS), :].astype(BF16)
        s_p = _dot_nt(qh, kc) * scale + bias_ref[h, :, 0:n_past]
        s_n = _dot_nt(qh, kn_ref[:, lanes]) * scale + bias_ref[h, :, n_past:n_past + n_new]
        m = jnp.maximum(jnp.max(s_p, axis=-1, keepdims=True), jnp.max(s_n, axis=-1, keepdims=True))
        p_p = jnp.exp(s_p - m)
        p_n = jnp.exp(s_n - m)
        l = jnp.sum(p_p, axis=-1, keepdims=True) + jnp.sum(p_n, axis=-1, keepdims=True)
        o = (_dot(p_p.astype(BF16), vc) + _dot(p_n.astype(BF16), vn_ref[:, lanes])) / l
        o_ref[:, lanes] = o.astype(BF16)


def _band_attention_step(z, cache_k, cache_v, bias, *, n_seq, n_new):
    n_past = cache_k.shape[1]
    kc = cache_k.reshape(n_seq, n_past * ATT_HEADS, ATT_DH)
    vc = cache_v.reshape(n_seq, n_past * ATT_HEADS, ATT_DH)
    qb, kb, vb = Z_QA // ATT_WIDTH, Z_KA // ATT_WIDTH, Z_VA // ATT_WIDTH
    kern = functools.partial(_band_step_kernel, n_new=n_new, n_past=n_past)
    return pl.pallas_call(
        kern,
        grid=(n_seq,),
        in_specs=[
            pl.BlockSpec((n_new, ATT_WIDTH), lambda b: (b, qb)),
            pl.BlockSpec((n_new, ATT_WIDTH), lambda b: (b, kb)),
            pl.BlockSpec((n_new, ATT_WIDTH), lambda b: (b, vb)),
            pl.BlockSpec((1, n_past * ATT_HEADS, ATT_DH), lambda b: (b, 0, 0)),
            pl.BlockSpec((1, n_past * ATT_HEADS, ATT_DH), lambda b: (b, 0, 0)),
            pl.BlockSpec(bias.shape, lambda b: (0, 0, 0)),
        ],
        out_specs=pl.BlockSpec((n_new, ATT_WIDTH), lambda b: (b, 0)),
        out_shape=jax.ShapeDtypeStruct((n_seq * n_new, ATT_WIDTH), BF16),
        compiler_params=_cparams(1),
        name="band_attention_step",
    )(z, z, z, kc, vc, bias)


def _merge_kernel(og_ref, oa_ref, mg_ref, ma_ref, x_ref, wg_ref, wa_ref, wo_ref, mb_ref,
                  g_ref, b_ref, h_ref, *, alpha):
    gate_g = _sigmoid(mg_ref[...].astype(F32) + mb_ref[0:1, :])
    gate_a = _sigmoid(ma_ref[...].astype(F32) + mb_ref[1:2, :])
    mixed = gate_g * _dot(og_ref[...], wg_ref[...]) + gate_a * _dot(oa_ref[...], wa_ref[...])
    mix = _dot(mixed.astype(BF16), wo_ref[...])
    h_ref[...] = _layer_norm(alpha * x_ref[...] + mix, g_ref[...], b_ref[...])


def _merge(o_gla, o_att, z, x2d, w_br_gla, w_br_att, w_out, merge_b, ln_g, ln_b, *, alpha):
    m = x2d.shape[0]
    bm = MERGE_BM
    mgb, mab = Z_MG // D_MODEL, Z_MA // D_MODEL
    resident = dict(pipeline_mode=pl.Buffered(1))
    kern = functools.partial(_merge_kernel, alpha=alpha)
    return pl.pallas_call(
        kern,
        grid=(m // bm,),
        in_specs=[
            pl.BlockSpec((bm, GLA_V), lambda i: (i, 0)),
            pl.BlockSpec((bm, ATT_WIDTH), lambda i: (i, 0)),
            pl.BlockSpec((bm, D_MODEL), lambda i: (i, mgb)),
            pl.BlockSpec((bm, D_MODEL), lambda i: (i, mab)),
            pl.BlockSpec((bm, D_MODEL), lambda i: (i, 0)),
            pl.BlockSpec((GLA_V, D_MODEL), lambda i: (0, 0), **resident),
            pl.BlockSpec((ATT_WIDTH, D_MODEL), lambda i: (0, 0), **resident),
            pl.BlockSpec((D_MODEL, D_MODEL), lambda i: (0, 0), **resident),
            pl.BlockSpec((2, D_MODEL), lambda i: (0, 0)),
            pl.BlockSpec((1, D_MODEL), lambda i: (0, 0)),
            pl.BlockSpec((1, D_MODEL), lambda i: (0, 0)),
        ],
        out_specs=pl.BlockSpec((bm, D_MODEL), lambda i: (i, 0)),
        out_shape=jax.ShapeDtypeStruct((m, D_MODEL), F32),
        compiler_params=_cparams(1),
        name="merge",
    )(o_gla, o_att, z, z, x2d, w_br_gla, w_br_att, w_out, merge_b, ln_g, ln_b)


def _ffn_kernel(h_ref, wu_ref, wg_ref, wd_ref, cw_ref, cb_ref, g_ref, b_ref, e0_ref, e1_ref,
                y_ref, ut_ref, hb_ref, carry_ref, *, alpha, seq_len, bm, whole_u):
    i = pl.program_id(0)
    j = pl.program_id(1)

    @pl.when(j == 0)
    def _():
        hb_ref[...] = h_ref[...].astype(BF16)

    hb = hb_ref[...]
    u = _dot(hb, wu_ref[...])
    gate = _dot(hb, wg_ref[...])
    row = lax.broadcasted_iota(jnp.int32, u.shape, 0)
    if seq_len >= bm:
        tiles_per_seq = seq_len // bm
        first = lax.rem(i, tiles_per_seq) == 0

        @pl.when(i == 0)
        def _():
            carry_ref[j] = jnp.zeros((8, u.shape[1]), F32)

        tail = carry_ref[j]
        p0 = jnp.where(first, e0_ref[0], tail[6:7, :])
        p1 = jnp.where(first, e1_ref[0], tail[7:8, :])
        carry_ref[j] = u[bm - 8:bm, :]
        is0 = row == 0
        is1 = row == 1
    else:
        p0 = e0_ref[...]
        p1 = e1_ref[...]
        pos = row & (seq_len - 1)
        is0 = pos == 0
        is1 = pos == 1
    u_m1 = jnp.where(is0, p1, pltpu.roll(u, 1, 0))
    u_m2 = jnp.where(is0, p0, jnp.where(is1, p1, pltpu.roll(u, 2, 0)))
    uc = cb_ref[...] + u_m2 * cw_ref[0:1, :] + u_m1 * cw_ref[1:2, :] + u * cw_ref[2:3, :]
    gelu = 0.5 * uc * (1.0 + jnp.tanh(math.sqrt(2.0 / math.pi) * (uc + 0.044715 * (uc * uc * uc))))
    part = _dot((gelu * gate).astype(BF16), wd_ref[...])
    if whole_u:
        ut_ref[...] = u
    else:
        ut_ref[0] = u[bm - 8:bm, :]

    @pl.when(j == 0)
    def _():
        y_ref[...] = part

    @pl.when(j > 0)
    def _():
        y_ref[...] += part

    @pl.when(j == pl.num_programs(1) - 1)
    def _():
        y_ref[...] = _layer_norm(alpha * h_ref[...] + y_ref[...], g_ref[...], b_ref[...])


def _ffn(h, w_up_u, w_up_g, w_down, conv_w, conv_b, ln_g, ln_b, conv_prev, *, alpha, n_seq, seq_len):
    m = h.shape[0]
    n_ff = D_FF_PAD // FFN_BF
    bm = FFN_BM
    if seq_len >= bm:
        tiles_per_seq = seq_len // bm
        e0 = conv_prev[:, 0:1, :]
        e1 = conv_prev[:, 1:2, :]
        e_spec = pl.BlockSpec((1, 1, FFN_BF), lambda i, j: (i // tiles_per_seq, 0, j))
        ut_shape = jax.ShapeDtypeStruct((m // bm, 8, D_FF_PAD), F32)
        ut_spec = pl.BlockSpec((1, 8, FFN_BF), lambda i, j: (i, 0, j))
        whole_u = False
    else:
        assert bm % seq_len == 0 and seq_len & (seq_len - 1) == 0 and seq_len >= CONV_W - 1
        e0 = jnp.repeat(conv_prev[:, 0, :], seq_len, axis=0)
        e1 = jnp.repeat(conv_prev[:, 1, :], seq_len, axis=0)
        e_spec = pl.BlockSpec((bm, FFN_BF), lambda i, j: (i, j))
        ut_shape = jax.ShapeDtypeStruct((m, D_FF_PAD), F32)
        ut_spec = pl.BlockSpec((bm, FFN_BF), lambda i, j: (i, j))
        whole_u = True
    kern = functools.partial(_ffn_kernel, alpha=alpha, seq_len=seq_len, bm=bm, whole_u=whole_u)
    return pl.pallas_call(
        kern,
        grid=(m // bm, n_ff),
        in_specs=[
            pl.BlockSpec((bm, D_MODEL), lambda i, j: (i, 0)),
            pl.BlockSpec((D_MODEL, FFN_BF), lambda i, j: (0, j)),
            pl.BlockSpec((D_MODEL, FFN_BF), lambda i, j: (0, j)),
            pl.BlockSpec((FFN_BF, D_MODEL), lambda i, j: (j, 0)),
            pl.BlockSpec((CONV_W, FFN_BF), lambda i, j: (0, j)),
            pl.BlockSpec((1, FFN_BF), lambda i, j: (0, j)),
            pl.BlockSpec((1, D_MODEL), lambda i, j: (0, 0)),
            pl.BlockSpec((1, D_MODEL), lambda i, j: (0, 0)),
            e_spec,
            e_spec,
        ],
        out_specs=[pl.BlockSpec((bm, D_MODEL), lambda i, j: (i, 0)), ut_spec],
        out_shape=[jax.ShapeDtypeStruct((m, D_MODEL), F32), ut_shape],
        scratch_shapes=[pltpu.VMEM((bm, D_MODEL), BF16),
                        pltpu.VMEM((n_ff, 8, FFN_BF), F32)],
        compiler_params=_cparams(2),
        name="ffn",
    )(h, w_up_u, w_up_g, w_down, conv_w, conv_b, ln_g, ln_b, e0, e1)


def _prepare_params(w_in, gla_gate_up, gla_gate_b, gla_norm_g, w_br_gla, w_br_att, w_out,
                    ln1_g, ln1_b, w_ffn_up, ffn_conv_w, ffn_conv_b, w_ffn_down, ln2_g, ln2_b):
    a0 = 2 * GLA_QK + 2 * GLA_V
    a1 = a0 + GLA_GATE_RANK
    att = w_in[:, a1:a1 + 3 * ATT_WIDTH]
    gates = w_in[:, a1 + 3 * ATT_WIDTH:]
    w_main = jnp.concatenate([w_in[:, :a0], gates, att], axis=1).astype(BF16)
    w_alo = jnp.pad(w_in[:, a0:a1], ((0, 0), (0, ALO_PAD - GLA_GATE_RANK))).astype(BF16)
    gup = jnp.pad(gla_gate_up, ((0, ALO_PAD - GLA_GATE_RANK), (0, 0))).astype(BF16)
    ffpad = D_FF_PAD - D_FF
    return dict(
        w_main=w_main, w_alo=w_alo, gup=gup,
        gate_b=gla_gate_b.reshape(1, GLA_QK), norm_g=gla_norm_g.reshape(1, GLA_V),
        w_br_gla=w_br_gla.astype(BF16), w_br_att=w_br_att.astype(BF16), w_out=w_out.astype(BF16),
        ln1_g=ln1_g.reshape(1, D_MODEL), ln1_b=ln1_b.reshape(1, D_MODEL),
        w_up_u=jnp.pad(w_ffn_up[:, :D_FF], ((0, 0), (0, ffpad))).astype(BF16),
        w_up_g=jnp.pad(w_ffn_up[:, D_FF:], ((0, 0), (0, ffpad))).astype(BF16),
        w_down=jnp.pad(w_ffn_down, ((0, ffpad), (0, 0))).astype(BF16),
        conv_w=jnp.pad(ffn_conv_w, ((0, 0), (0, ffpad))),
        conv_b=jnp.pad(ffn_conv_b.reshape(1, D_FF), ((0, 0), (0, ffpad))),
        ln2_g=ln2_g.reshape(1, D_MODEL), ln2_b=ln2_b.reshape(1, D_MODEL),
    )


def _layer(x, past_k, past_v, s0, conv_prev, p, table, merge_b, *, alpha):
    n_seq, seq_len, _ = x.shape
    m = n_seq * seq_len
    x2d = x.reshape(m, D_MODEL)
    z, alo = _inproj(x2d, p["w_main"], p["w_alo"])
    o_gla, s_fin = _gla(z, alo, p["gup"], p["gate_b"], p["norm_g"], s0, n_seq=n_seq, seq_len=seq_len)
    k_new = z[:, Z_KA:Z_KA + ATT_WIDTH].reshape(n_seq, seq_len, ATT_HEADS, ATT_DH)
    v_new = z[:, Z_VA:Z_VA + ATT_WIDTH].reshape(n_seq, seq_len, ATT_HEADS, ATT_DH)
    if past_k is None:
        bias = _rel_bias(table, n_q=ATT_GROUP, n_k=ATT_WINDOW, off=BAND_PAST, band=True)
        o_att = _band_attention(z, bias, n_seq=n_seq, seq_len=seq_len)
        keep = min(BAND_PAST, seq_len)
        k_new, v_new = k_new[:, seq_len - keep:], v_new[:, seq_len - keep:]
    else:
        n_past = past_k.shape[1]
        n_k = ((n_past + seq_len + V7X_LANES - 1) // V7X_LANES) * V7X_LANES
        bias = _rel_bias(table, n_q=seq_len, n_k=n_k, off=n_past, band=False)
        o_att = _band_attention_step(z, past_k, past_v, bias, n_seq=n_seq, n_new=seq_len)
    h = _merge(o_gla, o_att, z, x2d, p["w_br_gla"], p["w_br_att"], p["w_out"], merge_b,
               p["ln1_g"], p["ln1_b"], alpha=alpha)
    prev = jnp.pad(conv_prev, ((0, 0), (0, 0), (0, D_FF_PAD - D_FF)))
    y, ut = _ffn(h, p["w_up_u"], p["w_up_g"], p["w_down"], p["conv_w"], p["conv_b"],
                 p["ln2_g"], p["ln2_b"], prev, alpha=alpha, n_seq=n_seq, seq_len=seq_len)
    if seq_len >= FFN_BM:
        tiles_per_seq = seq_len // FFN_BM
        conv_new = ut.reshape(n_seq, tiles_per_seq, 8, D_FF_PAD)[:, -1, 8 - (CONV_W - 1):, :D_FF]
    else:
        conv_new = ut.reshape(n_seq, seq_len, D_FF_PAD)[:, seq_len - (CONV_W - 1):, :D_FF]
    return (y.reshape(n_seq, seq_len, D_MODEL), k_new.astype(F32), v_new.astype(F32), s_fin, conv_new)


def kernel(x_prompt, x_sample, cache_att_k, cache_att_v, state_gla, state_ffn_conv, w_in, gla_gate_up,
           gla_gate_b, gla_norm_g, att_rel_bias, merge_b, w_br_gla, w_br_att, w_out, ln1_g, ln1_b,
           w_ffn_up, ffn_conv_w, ffn_conv_b, w_ffn_down, ln2_g, ln2_b):
    depth = w_in.shape[0]
    alpha = (2.0 * depth) ** 0.25
    xp, xs = x_prompt, x_sample
    n_p = xp.shape[0]
    outs_p, outs_s = [], []
    for l in range(depth):
        p = _prepare_params(w_in[l], gla_gate_up[l], gla_gate_b[l], gla_norm_g[l], w_br_gla[l],
                            w_br_att[l], w_out[l], ln1_g[l], ln1_b[l], w_ffn_up[l], ffn_conv_w[l],
                            ffn_conv_b[l], w_ffn_down[l], ln2_g[l], ln2_b[l])
        s0_p = jnp.zeros((n_p, GLA_HEADS, GLA_DK, GLA_DV), F32)
        conv0_p = jnp.zeros((n_p, CONV_W - 1, D_FF), F32)
        xp, *rest_p = _layer(xp, None, None, s0_p, conv0_p, p, att_rel_bias[l], merge_b[l], alpha=alpha)
        outs_p.append(rest_p)
        xs, *rest_s = _layer(xs, cache_att_k[l], cache_att_v[l], state_gla[l], state_ffn_conv[l], p,
                             att_rel_bias[l], merge_b[l], alpha=alpha)
        outs_s.append(rest_s)
    stack = lambda outs, idx: jnp.stack([o[idx] for o in outs])
    return (xp, xs,
            stack(outs_p, 0), stack(outs_p, 1), stack(outs_p, 2), stack(outs_p, 3),
            stack(outs_s, 0), stack(outs_s, 1), stack(outs_s, 2), stack(outs_s, 3))
```

```python
import functools
import math

import jax
import jax.numpy as jnp
from jax import lax
from jax.experimental import pallas as pl
from jax.experimental.pallas import tpu as pltpu

F32 = jnp.float32
BF16 = jnp.bfloat16

D_MODEL = 2048
CHUNK = 64
CHUNK_SHIFT = CHUNK.bit_length() - 1
assert 1 << CHUNK_SHIFT == CHUNK
GLA_HEADS = 4
GLA_QK = D_MODEL // 2
GLA_V = D_MODEL
GLA_DK = GLA_QK // GLA_HEADS
GLA_DV = GLA_V // GLA_HEADS
GLA_GATE_RANK = 16
GLA_TAU = 16.0
ATT_HEADS = 8
ATT_DH = 128
ATT_WIDTH = ATT_HEADS * ATT_DH
BAND_CHUNKS = 8
BAND_PAST = BAND_CHUNKS * CHUNK
REL_CLIP = 128
D_FF = 5504
CONV_W = 3
LN_EPS = 1e-5
RMS_EPS = 1e-6
NEG_INF = -1e30

V7X_LANES = 128
V7X_VMEM_BYTES = 64 * 1024 * 1024
VMEM_LIMIT = V7X_VMEM_BYTES - 8 * 1024 * 1024

Z_QG = 0
Z_KG = Z_QG + GLA_QK
Z_VG = Z_KG + GLA_QK
Z_RG = Z_VG + GLA_V
Z_MG = Z_RG + GLA_V
Z_MA = Z_MG + D_MODEL
Z_QA = Z_MA + D_MODEL
Z_KA = Z_QA + ATT_WIDTH
Z_VA = Z_KA + ATT_WIDTH
Z_WIDTH = Z_VA + ATT_WIDTH
ALO_PAD = V7X_LANES

INPROJ_BM = 1024
INPROJ_BN = 1024
MERGE_BM = 256
GLA_ROWS = 512
FFN_BM = 512
FFN_BF = 512
FFN_HALF = 256
D_FF_PAD = ((D_FF + FFN_BF - 1) // FFN_BF) * FFN_BF
ATT_GROUP = 4 * CHUNK
ATT_WINDOW = ATT_GROUP + BAND_PAST


def _cparams(n_axes):
    return pltpu.CompilerParams(
        dimension_semantics=("arbitrary",) * n_axes, vmem_limit_bytes=VMEM_LIMIT)


def _dot(a, b):
    return jnp.dot(a, b, preferred_element_type=F32)


def _dot_nt(a, b):
    return lax.dot_general(a, b, (((1,), (1,)), ((), ())), preferred_element_type=F32)


def _dot_tn(a, b):
    return lax.dot_general(a, b, (((0,), (0,)), ((), ())), preferred_element_type=F32)


def _sigmoid(x):
    return 1.0 / (1.0 + jnp.exp(-x))


def _layer_norm(x, g, b):
    mu = jnp.mean(x, axis=-1, keepdims=True)
    xc = x - mu
    var = jnp.mean(xc * xc, axis=-1, keepdims=True)
    return xc * lax.rsqrt(var + LN_EPS) * g + b


def _inproj_kernel(x_ref, w_ref, walo_ref, z_ref, alo_ref, k32_ref, v32_ref, xb_ref, *,
                   keep_from, keep_rows, keep_every):
    i = pl.program_id(0)
    j = pl.program_id(1)

    @pl.when(j == 0)
    def _():
        xb_ref[...] = x_ref[...].astype(BF16)
        alo_ref[...] = _dot(xb_ref[...], walo_ref[...])

    res = _dot(xb_ref[...], w_ref[...])
    z_ref[...] = res.astype(BF16)

    keep_tile = lax.rem(i, keep_every) == keep_every - 1

    @pl.when(keep_tile & (j == Z_KA // INPROJ_BN))
    def _():
        k32_ref[...] = res[keep_from:keep_from + keep_rows, :]

    @pl.when(keep_tile & (j == Z_VA // INPROJ_BN))
    def _():
        v32_ref[...] = res[keep_from:keep_from + keep_rows, :]


def _inproj(x2d, w_main, w_alo, *, seq_len):
    assert INPROJ_BN == ATT_WIDTH and Z_KA % INPROJ_BN == 0 and Z_VA % INPROJ_BN == 0
    m = x2d.shape[0]
    bm = min(INPROJ_BM, m)
    keep = min(BAND_PAST, seq_len)
    if seq_len >= bm:
        assert seq_len % bm == 0 and keep <= bm
        keep_from, keep_rows, keep_every = bm - keep, keep, seq_len // bm
    else:
        assert keep == seq_len and bm % seq_len == 0
        keep_from, keep_rows, keep_every = 0, bm, 1
    n_keep = m // bm // keep_every * keep_rows
    kern = functools.partial(_inproj_kernel, keep_from=keep_from, keep_rows=keep_rows,
                             keep_every=keep_every)
    keep_spec = pl.BlockSpec((keep_rows, ATT_WIDTH), lambda i, j: (i // keep_every, 0))
    return pl.pallas_call(
        kern,
        grid=(m // bm, Z_WIDTH // INPROJ_BN),
        in_specs=[
            pl.BlockSpec((bm, D_MODEL), lambda i, j: (i, 0)),
            pl.BlockSpec((D_MODEL, INPROJ_BN), lambda i, j: (0, j)),
            pl.BlockSpec((D_MODEL, ALO_PAD), lambda i, j: (0, 0)),
        ],
        out_specs=[
            pl.BlockSpec((bm, INPROJ_BN), lambda i, j: (i, j)),
            pl.BlockSpec((bm, ALO_PAD), lambda i, j: (i, 0)),
            keep_spec,
            keep_spec,
        ],
        out_shape=[
            jax.ShapeDtypeStruct((m, Z_WIDTH), BF16),
            jax.ShapeDtypeStruct((m, ALO_PAD), F32),
            jax.ShapeDtypeStruct((n_keep, ATT_WIDTH), F32),
            jax.ShapeDtypeStruct((n_keep, ATT_WIDTH), F32),
        ],
        scratch_shapes=[pltpu.VMEM((bm, D_MODEL), BF16)],
        compiler_params=_cparams(2),
        name="inproj",
    )(x2d, w_main, w_alo)


def _gla_kernel(q_ref, k_ref, v_ref, r_ref, alo_ref, gup_ref, gb_ref, ng_ref, s0_ref,
                o_ref, sfin_ref, st_ref, *, rows_per_step, blk):
    t = pl.program_id(1)

    @pl.when(t == 0)
    def _():
        for h in range(GLA_HEADS):
            st_ref[h] = s0_ref[0, h].T

    row = lax.broadcasted_iota(jnp.int32, (blk, blk), 0)
    col = lax.broadcasted_iota(jnp.int32, (blk, blk), 1)
    causal = row >= col
    tril = jnp.where(causal, 1.0, 0.0).astype(BF16)

    def body(i, carry):
        r0 = pl.multiple_of(i * blk, blk)
        rows = pl.ds(r0, blk)
        x = _dot(alo_ref[rows, :].astype(BF16), gup_ref[...]) + gb_ref[...]
        log_a = -(jnp.maximum(-x, 0.0) + jnp.log(1.0 + jnp.exp(-jnp.abs(x)))) * (1.0 / GLA_TAU)
        hi = log_a.astype(BF16)
        rem = log_a - hi.astype(F32)
        mid = rem.astype(BF16)
        lo = (rem - mid.astype(F32)).astype(BF16)
        b = _dot(tril, hi) + _dot(tril, mid) + _dot(tril, lo)
        b_last = b[blk - 1:blk, :]
        qf = q_ref[rows, :].astype(F32) * (GLA_DK ** -0.5)
        kf = k_ref[rows, :].astype(F32)
        q_t = (qf * jnp.exp(b)).astype(BF16)
        k_t = (kf * jnp.exp(-b)).astype(BF16)
        k_dec = (kf * jnp.exp(b_last - b)).astype(BF16)
        decay = jnp.exp(b_last)
        for h in range(GLA_HEADS):
            lk = slice(h * GLA_DK, (h + 1) * GLA_DK)
            lv = slice(h * GLA_DV, (h + 1) * GLA_DV)
            att = jnp.where(causal, _dot_nt(q_t[:, lk], k_t[:, lk]), 0.0).astype(BF16)
            vb = v_ref[rows, lv]
            st = st_ref[h]
            o = _dot(att, vb) + _dot_nt(q_t[:, lk], st.astype(BF16))
            st_ref[h] = st * decay[:, lk] + _dot_tn(vb, k_dec[:, lk])
            ms = jnp.mean(o * o, axis=-1, keepdims=True)
            on = o * lax.rsqrt(ms + RMS_EPS) * ng_ref[:, lv]
            rf = r_ref[rows, lv].astype(F32)
            o_ref[rows, lv] = (on * (rf * _sigmoid(rf))).astype(BF16)
        return carry

    n_blk = rows_per_step // blk
    lax.fori_loop(0, n_blk, body, 0, unroll=2 if n_blk % 2 == 0 else 1)

    @pl.when(t == pl.num_programs(1) - 1)
    def _():
        for h in range(GLA_HEADS):
            sfin_ref[0, h] = st_ref[h].T


def _gla(z, alo, gup_pad, gate_b, norm_g, s0, *, n_seq, seq_len):
    blk = min(CHUNK, seq_len)
    rt = min(GLA_ROWS, seq_len)
    n_t = seq_len // rt
    kern = functools.partial(_gla_kernel, rows_per_step=rt, blk=blk)
    row_block = lambda b, t: b * n_t + t
    return pl.pallas_call(
        kern,
        grid=(n_seq, n_t),
        in_specs=[
            pl.BlockSpec((rt, GLA_QK), lambda b, t: (row_block(b, t), Z_QG // GLA_QK)),
            pl.BlockSpec((rt, GLA_QK), lambda b, t: (row_block(b, t), Z_KG // GLA_QK)),
            pl.BlockSpec((rt, GLA_V), lambda b, t: (row_block(b, t), Z_VG // GLA_V)),
            pl.BlockSpec((rt, GLA_V), lambda b, t: (row_block(b, t), Z_RG // GLA_V)),
            pl.BlockSpec((rt, ALO_PAD), lambda b, t: (row_block(b, t), 0)),
            pl.BlockSpec((ALO_PAD, GLA_QK), lambda b, t: (0, 0)),
            pl.BlockSpec((1, GLA_QK), lambda b, t: (0, 0)),
            pl.BlockSpec((1, GLA_V), lambda b, t: (0, 0)),
            pl.BlockSpec((1, GLA_HEADS, GLA_DK, GLA_DV), lambda b, t: (b, 0, 0, 0)),
        ],
        out_specs=[
            pl.BlockSpec((rt, GLA_V), lambda b, t: (row_block(b, t), 0)),
            pl.BlockSpec((1, GLA_HEADS, GLA_DK, GLA_DV), lambda b, t: (b, 0, 0, 0)),
        ],
        out_shape=[
            jax.ShapeDtypeStruct((n_seq * seq_len, GLA_V), BF16),
            jax.ShapeDtypeStruct((n_seq, GLA_HEADS, GLA_DK, GLA_DV), F32),
        ],
        scratch_shapes=[pltpu.VMEM((GLA_HEADS, GLA_DV, GLA_DK), F32)],
        compiler_params=_cparams(2),
        name="gla",
    )(z, z, z, z, alo, gup_pad, gate_b, norm_g, s0)


def _rel_bias_kernel(tab_ref, out_ref, *, n_q, n_k, off, band):
    h = pl.program_id(0)
    k8 = lax.broadcasted_iota(jnp.int32, (8, n_k), 1)
    rel0 = jnp.clip(off - k8, -REL_CLIP, REL_CLIP) + REL_CLIP

    def body(r, acc):
        return jnp.where(rel0 == r, tab_ref[h, r], acc)

    row0 = lax.fori_loop(0, 2 * REL_CLIP + 1, body, jnp.zeros((8, n_k), F32))
    full = jnp.broadcast_to(row0[0:1, :], (n_q, n_k))
    rolled = pltpu.roll(full, 0, 1, stride=1, stride_axis=0)
    q = lax.broadcasted_iota(jnp.int32, (n_q, n_k), 0)
    k = lax.broadcasted_iota(jnp.int32, (n_q, n_k), 1)
    bias = jnp.where(k < q, tab_ref[h, 2 * REL_CLIP], rolled)
    if band:
        dc = (k >> CHUNK_SHIFT) - (q >> CHUNK_SHIFT)
        bias = jnp.where(dc < 0, NEG_INF, jnp.where(dc > BAND_CHUNKS, NEG_INF, bias))
    out_ref[0] = bias


def _rel_bias(table, *, n_q, n_k, off, band):
    assert off >= REL_CLIP and n_k % V7X_LANES == 0
    kern = functools.partial(_rel_bias_kernel, n_q=n_q, n_k=n_k, off=off, band=band)
    return pl.pallas_call(
        kern,
        grid=(ATT_HEADS,),
        in_specs=[pl.BlockSpec(memory_space=pltpu.SMEM)],
        out_specs=pl.BlockSpec((1, n_q, n_k), lambda h: (h, 0, 0)),
        out_shape=jax.ShapeDtypeStruct((ATT_HEADS, n_q, n_k), F32),
        compiler_params=_cparams(1),
        name="rel_bias",
    )(table)


def _band_kernel(q_ref, k_ref, v_ref, bias_ref, o_ref, kp_ref, vp_ref, *, seq_len):
    zeros = jnp.zeros((BAND_PAST, ATT_DH), BF16)
    kp_ref[0:BAND_PAST, :] = zeros
    vp_ref[0:BAND_PAST, :] = zeros
    kp_ref[BAND_PAST:, :] = k_ref[...]
    vp_ref[BAND_PAST:, :] = v_ref[...]
    kk = lax.broadcasted_iota(jnp.int32, (ATT_GROUP, ATT_WINDOW), 1)

    def group(q0, before_start):
        qg = q_ref[pl.ds(q0, ATT_GROUP), :]
        kw = kp_ref[pl.ds(q0, ATT_WINDOW), :]
        vw = vp_ref[pl.ds(q0, ATT_WINDOW), :]
        s = _dot_nt(qg, kw) * (ATT_DH ** -0.5) + bias_ref[0]
        if before_start:
            s = jnp.where(kk >= BAND_PAST - q0, s, NEG_INF)
        m = jnp.max(s, axis=-1, keepdims=True)
        p = jnp.exp(s - m)
        l = jnp.sum(p, axis=-1, keepdims=True)
        o = _dot(p.astype(BF16), vw) / l
        o_ref[pl.ds(q0, ATT_GROUP), :] = o.astype(BF16)

    n_groups = seq_len // ATT_GROUP
    n_head = min(-(-BAND_PAST // ATT_GROUP), n_groups)
    for g in range(n_head):
        group(g * ATT_GROUP, True)

    def body(g, carry):
        group(pl.multiple_of(g * ATT_GROUP, ATT_GROUP), False)
        return carry

    lax.fori_loop(n_head, n_groups, body, 0, unroll=2)


def _band_attention(z, bias, *, n_seq, seq_len):
    qb, kb, vb = Z_QA // ATT_DH, Z_KA // ATT_DH, Z_VA // ATT_DH
    kern = functools.partial(_band_kernel, seq_len=seq_len)
    return pl.pallas_call(
        kern,
        grid=(ATT_HEADS, n_seq),
        in_specs=[
            pl.BlockSpec((seq_len, ATT_DH), lambda h, b: (b, qb + h)),
            pl.BlockSpec((seq_len, ATT_DH), lambda h, b: (b, kb + h)),
            pl.BlockSpec((seq_len, ATT_DH), lambda h, b: (b, vb + h)),
            pl.BlockSpec((1, ATT_GROUP, ATT_WINDOW), lambda h, b: (h, 0, 0)),
        ],
        out_specs=pl.BlockSpec((seq_len, ATT_DH), lambda h, b: (b, h)),
        out_shape=jax.ShapeDtypeStruct((n_seq * seq_len, ATT_WIDTH), BF16),
        scratch_shapes=[pltpu.VMEM((seq_len + BAND_PAST, ATT_DH), BF16),
                        pltpu.VMEM((seq_len + BAND_PAST, ATT_DH), BF16)],
        compiler_params=_cparams(2),
        name="band_attention",
    )(z, z, z, bias)


def _band_step_kernel(q_ref, kn_ref, vn_ref, kc_ref, vc_ref, bias_ref, o_ref, *, n_new, n_past):
    scale = ATT_DH ** -0.5
    for h in range(ATT_HEADS):
        lanes = slice(h * ATT_DH, (h + 1) * ATT_DH)
        qh = q_ref[:, lanes]
        kc = kc_ref[0, pl.ds(h, n_past, stride=ATT_HEADS), :].astype(BF16)
        vc = vc_ref[0, pl.ds(h, n_past, stride=ATT_HEADS), :].astype(BF16)
        s_p = _dot_nt(qh, kc) * scale + bias_ref[h, :, 0:n_past]
        s_n = _dot_nt(qh, kn_ref[:, lanes]) * scale + bias_ref[h, :, n_past:n_past + n_new]
        m = jnp.maximum(jnp.max(s_p, axis=-1, keepdims=True), jnp.max(s_n, axis=-1, keepdims=True))
        p_p = jnp.exp(s_p - m)
        p_n = jnp.exp(s_n - m)
        l = jnp.sum(p_p, axis=-1, keepdims=True) + jnp.sum(p_n, axis=-1, keepdims=True)
        o = (_dot(p_p.astype(BF16), vc) + _dot(p_n.astype(BF16), vn_ref[:, lanes])) / l
        o_ref[:, lanes] = o.astype(BF16)


def _band_attention_step(z, cache_k, cache_v, bias, *, n_seq, n_new):
    n_past = cache_k.shape[1]
    kc = cache_k.reshape(n_seq, n_past * ATT_HEADS, ATT_DH)
    vc = cache_v.reshape(n_seq, n_past * ATT_HEADS, ATT_DH)
    qb, kb, vb = Z_QA // ATT_WIDTH, Z_KA // ATT_WIDTH, Z_VA // ATT_WIDTH
    kern = functools.partial(_band_step_kernel, n_new=n_new, n_past=n_past)
    return pl.pallas_call(
        kern,
        grid=(n_seq,),
        in_specs=[
            pl.BlockSpec((n_new, ATT_WIDTH), lambda b: (b, qb)),
            pl.BlockSpec((n_new, ATT_WIDTH), lambda b: (b, kb)),
            pl.BlockSpec((n_new, ATT_WIDTH), lambda b: (b, vb)),
            pl.BlockSpec((1, n_past * ATT_HEADS, ATT_DH), lambda b: (b, 0, 0)),
            pl.BlockSpec((1, n_past * ATT_HEADS, ATT_DH), lambda b: (b, 0, 0)),
            pl.BlockSpec(bias.shape, lambda b: (0, 0, 0)),
        ],
        out_specs=pl.BlockSpec((n_new, ATT_WIDTH), lambda b: (b, 0)),
        out_shape=jax.ShapeDtypeStruct((n_seq * n_new, ATT_WIDTH), BF16),
        compiler_params=_cparams(1),
        name="band_attention_step",
    )(z, z, z, kc, vc, bias)


def _merge_kernel(og_ref, oa_ref, mg_ref, ma_ref, x_ref, wg_ref, wa_ref, wo_ref, mb_ref,
                  g_ref, b_ref, h_ref, *, alpha):
    gate_g = _sigmoid(mg_ref[...].astype(F32) + mb_ref[0:1, :])
    gate_a = _sigmoid(ma_ref[...].astype(F32) + mb_ref[1:2, :])
    mixed = gate_g * _dot(og_ref[...], wg_ref[...]) + gate_a * _dot(oa_ref[...], wa_ref[...])
    mix = _dot(mixed.astype(BF16), wo_ref[...])
    h_ref[...] = _layer_norm(alpha * x_ref[...] + mix, g_ref[...], b_ref[...])


def _merge(o_gla, o_att, z, x2d, w_br_gla, w_br_att, w_out, merge_b, ln_g, ln_b, *, alpha):
    m = x2d.shape[0]
    bm = MERGE_BM
    mgb, mab = Z_MG // D_MODEL, Z_MA // D_MODEL
    resident = dict(pipeline_mode=pl.Buffered(1))
    kern = functools.partial(_merge_kernel, alpha=alpha)
    return pl.pallas_call(
        kern,
        grid=(m // bm,),
        in_specs=[
            pl.BlockSpec((bm, GLA_V), lambda i: (i, 0)),
            pl.BlockSpec((bm, ATT_WIDTH), lambda i: (i, 0)),
            pl.BlockSpec((bm, D_MODEL), lambda i: (i, mgb)),
            pl.BlockSpec((bm, D_MODEL), lambda i: (i, mab)),
            pl.BlockSpec((bm, D_MODEL), lambda i: (i, 0)),
            pl.BlockSpec((GLA_V, D_MODEL), lambda i: (0, 0), **resident),
            pl.BlockSpec((ATT_WIDTH, D_MODEL), lambda i: (0, 0), **resident),
            pl.BlockSpec((D_MODEL, D_MODEL), lambda i: (0, 0), **resident),
            pl.BlockSpec((2, D_MODEL), lambda i: (0, 0)),
            pl.BlockSpec((1, D_MODEL), lambda i: (0, 0)),
            pl.BlockSpec((1, D_MODEL), lambda i: (0, 0)),
        ],
        out_specs=pl.BlockSpec((bm, D_MODEL), lambda i: (i, 0)),
        out_shape=jax.ShapeDtypeStruct((m, D_MODEL), F32),
        compiler_params=_cparams(1),
        name="merge",
    )(o_gla, o_att, z, z, x2d, w_br_gla, w_br_att, w_out, merge_b, ln_g, ln_b)


def _ffn_kernel(h_ref, wu_ref, wg_ref, wd_ref, cw_ref, cb_ref, g_ref, b_ref, e0_ref, e1_ref,
                y_ref, ut_ref, hb_ref, carry_ref, *, alpha, seq_len, bm, whole_u):
    i = pl.program_id(0)
    j = pl.program_id(1)

    @pl.when(j == 0)
    def _():
        hb_ref[...] = h_ref[...].astype(BF16)
        y_ref[...] = jnp.zeros(y_ref.shape, F32)

    if seq_len >= bm:
        @pl.when(i == 0)
        def _():
            carry_ref[j] = jnp.zeros(carry_ref.shape[1:], F32)

    hb = hb_ref[...]
    row = lax.broadcasted_iota(jnp.int32, (bm, FFN_HALF), 0)
    part = None
    for c in range(FFN_BF // FFN_HALF):
        cols = slice(c * FFN_HALF, (c + 1) * FFN_HALF)
        u = _dot(hb, wu_ref[:, cols])
        gate = _dot(hb, wg_ref[:, cols])
        if seq_len >= bm:
            first = lax.rem(i, seq_len // bm) == 0
            p0 = jnp.where(first, e0_ref[0, :, cols], carry_ref[j, 6:7, cols])
            p1 = jnp.where(first, e1_ref[0, :, cols], carry_ref[j, 7:8, cols])
            carry_ref[j, :, cols] = u[bm - 8:bm, :]
            is0 = row == 0
            is1 = row == 1
        else:
            p0 = e0_ref[:, cols]
            p1 = e1_ref[:, cols]
            pos = row & (seq_len - 1)
            is0 = pos == 0
            is1 = pos == 1
        u_m1 = jnp.where(is0, p1, pltpu.roll(u, 1, 0))
        u_m2 = jnp.where(is0, p0, jnp.where(is1, p1, pltpu.roll(u, 2, 0)))
        uc = (cb_ref[:, cols] + u_m2 * cw_ref[0:1, cols] + u_m1 * cw_ref[1:2, cols]
              + u * cw_ref[2:3, cols])
        gelu = 0.5 * uc * (1.0 + jnp.tanh(math.sqrt(2.0 / math.pi) * (uc + 0.044715 * (uc * uc * uc))))
        d = _dot((gelu * gate).astype(BF16), wd_ref[cols, :])
        part = d if part is None else part + d
        if whole_u:
            ut_ref[:, cols] = u
        else:
            ut_ref[0, :, cols] = u[bm - 8:bm, :]
    y_ref[...] += part

    @pl.when(j == pl.num_programs(1) - 1)
    def _():
        y_ref[...] = _layer_norm(alpha * h_ref[...] + y_ref[...], g_ref[...], b_ref[...])


def _ffn(h, w_up_u, w_up_g, w_down, conv_w, conv_b, ln_g, ln_b, conv_prev, *, alpha, n_seq, seq_len):
    m = h.shape[0]
    n_ff = D_FF_PAD // FFN_BF
    bm = FFN_BM
    if seq_len >= bm:
        tiles_per_seq = seq_len // bm
        e0 = conv_prev[:, 0:1, :]
        e1 = conv_prev[:, 1:2, :]
        e_spec = pl.BlockSpec((1, 1, FFN_BF), lambda i, j: (i // tiles_per_seq, 0, j))
        ut_shape = jax.ShapeDtypeStruct((m // bm, 8, D_FF_PAD), F32)
        ut_spec = pl.BlockSpec((1, 8, FFN_BF), lambda i, j: (i, 0, j))
        whole_u = False
    else:
        assert bm % seq_len == 0 and seq_len & (seq_len - 1) == 0 and seq_len >= CONV_W - 1
        e0 = jnp.repeat(conv_prev[:, 0, :], seq_len, axis=0)
        e1 = jnp.repeat(conv_prev[:, 1, :], seq_len, axis=0)
        e_spec = pl.BlockSpec((bm, FFN_BF), lambda i, j: (i, j))
        ut_shape = jax.ShapeDtypeStruct((m, D_FF_PAD), F32)
        ut_spec = pl.BlockSpec((bm, FFN_BF), lambda i, j: (i, j))
        whole_u = True
    kern = functools.partial(_ffn_kernel, alpha=alpha, seq_len=seq_len, bm=bm, whole_u=whole_u)
    return pl.pallas_call(
        kern,
        grid=(m // bm, n_ff),
        in_specs=[
            pl.BlockSpec((bm, D_MODEL), lambda i, j: (i, 0)),
            pl.BlockSpec((D_MODEL, FFN_BF), lambda i, j: (0, j)),
            pl.BlockSpec((D_MODEL, FFN_BF), lambda i, j: (0, j)),
            pl.BlockSpec((FFN_BF, D_MODEL), lambda i, j: (j, 0)),
            pl.BlockSpec((CONV_W, FFN_BF), lambda i, j: (0, j)),
            pl.BlockSpec((1, FFN_BF), lambda i, j: (0, j)),
            pl.BlockSpec((1, D_MODEL), lambda i, j: (0, 0)),
            pl.BlockSpec((1, D_MODEL), lambda i, j: (0, 0)),
            e_spec,
            e_spec,
        ],
        out_specs=[pl.BlockSpec((bm, D_MODEL), lambda i, j: (i, 0)), ut_spec],
        out_shape=[jax.ShapeDtypeStruct((m, D_MODEL), F32), ut_shape],
        scratch_shapes=[pltpu.VMEM((bm, D_MODEL), BF16),
                        pltpu.VMEM((n_ff, 8, FFN_BF), F32)],
        compiler_params=_cparams(2),
        name="ffn",
    )(h, w_up_u, w_up_g, w_down, conv_w, conv_b, ln_g, ln_b, e0, e1)


def _prepare_params(w_in, gla_gate_up, gla_gate_b, gla_norm_g, w_br_gla, w_br_att, w_out,
                    ln1_g, ln1_b, w_ffn_up, ffn_conv_w, ffn_conv_b, w_ffn_down, ln2_g, ln2_b):
    a0 = 2 * GLA_QK + 2 * GLA_V
    a1 = a0 + GLA_GATE_RANK
    att = w_in[:, a1:a1 + 3 * ATT_WIDTH]
    gates = w_in[:, a1 + 3 * ATT_WIDTH:]
    w_main = jnp.concatenate([w_in[:, :a0], gates, att], axis=1).astype(BF16)
    w_alo = jnp.pad(w_in[:, a0:a1], ((0, 0), (0, ALO_PAD - GLA_GATE_RANK))).astype(BF16)
    gup = jnp.pad(gla_gate_up, ((0, ALO_PAD - GLA_GATE_RANK), (0, 0))).astype(BF16)
    ffpad = D_FF_PAD - D_FF
    return dict(
        w_main=w_main, w_alo=w_alo, gup=gup,
        gate_b=gla_gate_b.reshape(1, GLA_QK), norm_g=gla_norm_g.reshape(1, GLA_V),
        w_br_gla=w_br_gla.astype(BF16), w_br_att=w_br_att.astype(BF16), w_out=w_out.astype(BF16),
        ln1_g=ln1_g.reshape(1, D_MODEL), ln1_b=ln1_b.reshape(1, D_MODEL),
        w_up_u=jnp.pad(w_ffn_up[:, :D_FF], ((0, 0), (0, ffpad))).astype(BF16),
        w_up_g=jnp.pad(w_ffn_up[:, D_FF:], ((0, 0), (0, ffpad))).astype(BF16),
        w_down=jnp.pad(w_ffn_down, ((0, ffpad), (0, 0))).astype(BF16),
        conv_w=jnp.pad(ffn_conv_w, ((0, 0), (0, ffpad))),
        conv_b=jnp.pad(ffn_conv_b.reshape(1, D_FF), ((0, 0), (0, ffpad))),
        ln2_g=ln2_g.reshape(1, D_MODEL), ln2_b=ln2_b.reshape(1, D_MODEL),
    )


def _layer(x, past_k, past_v, s0, conv_prev, p, table, merge_b, *, alpha):
    n_seq, seq_len, _ = x.shape
    m = n_seq * seq_len
    x2d = x.reshape(m, D_MODEL)
    z, alo, k_new, v_new = _inproj(x2d, p["w_main"], p["w_alo"], seq_len=seq_len)
    o_gla, s_fin = _gla(z, alo, p["gup"], p["gate_b"], p["norm_g"], s0, n_seq=n_seq, seq_len=seq_len)
    keep = min(BAND_PAST, seq_len)
    k_new = k_new.reshape(n_seq, keep, ATT_HEADS, ATT_DH)
    v_new = v_new.reshape(n_seq, keep, ATT_HEADS, ATT_DH)
    if past_k is None:
        bias = _rel_bias(table, n_q=ATT_GROUP, n_k=ATT_WINDOW, off=BAND_PAST, band=True)
        o_att = _band_attention(z, bias, n_seq=n_seq, seq_len=seq_len)
    else:
        n_past = past_k.shape[1]
        n_k = ((n_past + seq_len + V7X_LANES - 1) // V7X_LANES) * V7X_LANES
        bias = _rel_bias(table, n_q=seq_len, n_k=n_k, off=n_past, band=False)
        o_att = _band_attention_step(z, past_k, past_v, bias, n_seq=n_seq, n_new=seq_len)
    h = _merge(o_gla, o_att, z, x2d, p["w_br_gla"], p["w_br_att"], p["w_out"], merge_b,
               p["ln1_g"], p["ln1_b"], alpha=alpha)
    prev = jnp.pad(conv_prev, ((0, 0), (0, 0), (0, D_FF_PAD - D_FF)))
    y, ut = _ffn(h, p["w_up_u"], p["w_up_g"], p["w_down"], p["conv_w"], p["conv_b"],
                 p["ln2_g"], p["ln2_b"], prev, alpha=alpha, n_seq=n_seq, seq_len=seq_len)
    if seq_len >= FFN_BM:
        tiles_per_seq = seq_len // FFN_BM
        conv_new = ut.reshape(n_seq, tiles_per_seq, 8, D_FF_PAD)[:, -1, 8 - (CONV_W - 1):, :D_FF]
    else:
        conv_new = ut.reshape(n_seq, seq_len, D_FF_PAD)[:, seq_len - (CONV_W - 1):, :D_FF]
    return (y.reshape(n_seq, seq_len, D_MODEL), k_new, v_new, s_fin, conv_new)


def kernel(x_prompt, x_sample, cache_att_k, cache_att_v, state_gla, state_ffn_conv, w_in, gla_gate_up,
           gla_gate_b, gla_norm_g, att_rel_bias, merge_b, w_br_gla, w_br_att, w_out, ln1_g, ln1_b,
           w_ffn_up, ffn_conv_w, ffn_conv_b, w_ffn_down, ln2_g, ln2_b):
    depth = w_in.shape[0]
    alpha = (2.0 * depth) ** 0.25
    xp, xs = x_prompt, x_sample
    n_p = xp.shape[0]
    outs_p, outs_s = [], []
    for l in range(depth):
        p = _prepare_params(w_in[l], gla_gate_up[l], gla_gate_b[l], gla_norm_g[l], w_br_gla[l],
                            w_br_att[l], w_out[l], ln1_g[l], ln1_b[l], w_ffn_up[l], ffn_conv_w[l],
                            ffn_conv_b[l], w_ffn_down[l], ln2_g[l], ln2_b[l])
        s0_p = jnp.zeros((n_p, GLA_HEADS, GLA_DK, GLA_DV), F32)
        conv0_p = jnp.zeros((n_p, CONV_W - 1, D_FF), F32)
        xp, *rest_p = _layer(xp, None, None, s0_p, conv0_p, p, att_rel_bias[l], merge_b[l], alpha=alpha)
        outs_p.append(rest_p)
        xs, *rest_s = _layer(xs, cache_att_k[l], cache_att_v[l], state_gla[l], state_ffn_conv[l], p,
                             att_rel_bias[l], merge_b[l], alpha=alpha)
        outs_s.append(rest_s)
    stack = lambda outs, idx: jnp.stack([o[idx] for o in outs])
    return (xp, xs,
            stack(outs_p, 0), stack(outs_p, 1), stack(outs_p, 2), stack(outs_p, 3),
            stack(outs_s, 0), stack(outs_s, 1), stack(outs_s, 2), stack(outs_s, 3))
```

```python
import functools
import math

import jax
import jax.numpy as jnp
from jax import lax
from jax.experimental import pallas as pl
from jax.experimental.pallas import tpu as pltpu

F32 = jnp.float32
BF16 = jnp.bfloat16

D_MODEL = 2048
CHUNK = 64
CHUNK_SHIFT = CHUNK.bit_length() - 1
assert 1 << CHUNK_SHIFT == CHUNK
GLA_HEADS = 4
GLA_QK = D_MODEL // 2
GLA_V = D_MODEL
GLA_DK = GLA_QK // GLA_HEADS
GLA_DV = GLA_V // GLA_HEADS
GLA_GATE_RANK = 16
GLA_TAU = 16.0
ATT_HEADS = 8
ATT_DH = 128
ATT_WIDTH = ATT_HEADS * ATT_DH
BAND_CHUNKS = 8
BAND_PAST = BAND_CHUNKS * CHUNK
REL_CLIP = 128
D_FF = 5504
CONV_W = 3
LN_EPS = 1e-5
RMS_EPS = 1e-6
NEG_INF = -1e30

V7X_LANES = 128
V7X_VMEM_BYTES = 64 * 1024 * 1024
VMEM_LIMIT = V7X_VMEM_BYTES - 8 * 1024 * 1024

Z_QG = 0
Z_KG = Z_QG + GLA_QK
Z_VG = Z_KG + GLA_QK
Z_RG = Z_VG + GLA_V
Z_MG = Z_RG + GLA_V
Z_MA = Z_MG + D_MODEL
Z_QA = Z_MA + D_MODEL
Z_KA = Z_QA + ATT_WIDTH
Z_VA = Z_KA + ATT_WIDTH
Z_WIDTH = Z_VA + ATT_WIDTH
ALO_PAD = V7X_LANES

INPROJ_BM = 1024
INPROJ_BN = 1024
MERGE_BM = 256
GLA_ROWS = 512
GLA_SUPER = 4 * CHUNK
FFN_BM = 512
FFN_BF = 512
FFN_HALF = 256
D_FF_PAD = ((D_FF + FFN_BF - 1) // FFN_BF) * FFN_BF
ATT_GROUP = 4 * CHUNK
ATT_WINDOW = ATT_GROUP + BAND_PAST


def _cparams(n_axes):
    return pltpu.CompilerParams(
        dimension_semantics=("arbitrary",) * n_axes, vmem_limit_bytes=VMEM_LIMIT)


def _dot(a, b):
    return jnp.dot(a, b, preferred_element_type=F32)


def _dot_nt(a, b):
    return lax.dot_general(a, b, (((1,), (1,)), ((), ())), preferred_element_type=F32)


def _dot_tn(a, b):
    return lax.dot_general(a, b, (((0,), (0,)), ((), ())), preferred_element_type=F32)


LOG2_E = 1.4426950408889634


def _top_bits(x):
    bits = lax.bitcast_convert_type(x, jnp.uint32) & jnp.uint32(0xFFFF0000)
    return lax.bitcast_convert_type(bits, F32)


def _sigmoid(x):
    return 1.0 / (1.0 + jnp.exp(-x))


def _layer_norm(x, g, b):
    mu = jnp.mean(x, axis=-1, keepdims=True)
    xc = x - mu
    var = jnp.mean(xc * xc, axis=-1, keepdims=True)
    return xc * lax.rsqrt(var + LN_EPS) * g + b


def _inproj_kernel(x_ref, w_ref, walo_ref, z_ref, alo_ref, k32_ref, v32_ref, xb_ref, *,
                   keep_from, keep_rows, keep_every):
    i = pl.program_id(0)
    j = pl.program_id(1)

    @pl.when(j == 0)
    def _():
        xb_ref[...] = x_ref[...].astype(BF16)
        alo_ref[...] = _dot(xb_ref[...], walo_ref[...])

    res = _dot(xb_ref[...], w_ref[...])
    z_ref[...] = res.astype(BF16)

    keep_tile = lax.rem(i, keep_every) == keep_every - 1

    @pl.when(keep_tile & (j == Z_KA // INPROJ_BN))
    def _():
        k32_ref[...] = res[keep_from:keep_from + keep_rows, :]

    @pl.when(keep_tile & (j == Z_VA // INPROJ_BN))
    def _():
        v32_ref[...] = res[keep_from:keep_from + keep_rows, :]


def _inproj(x2d, w_main, w_alo, *, seq_len):
    assert INPROJ_BN == ATT_WIDTH and Z_KA % INPROJ_BN == 0 and Z_VA % INPROJ_BN == 0
    m = x2d.shape[0]
    bm = min(INPROJ_BM, m)
    keep = min(BAND_PAST, seq_len)
    if seq_len >= bm:
        assert seq_len % bm == 0 and keep <= bm
        keep_from, keep_rows, keep_every = bm - keep, keep, seq_len // bm
    else:
        assert keep == seq_len and bm % seq_len == 0
        keep_from, keep_rows, keep_every = 0, bm, 1
    n_keep = m // bm // keep_every * keep_rows
    kern = functools.partial(_inproj_kernel, keep_from=keep_from, keep_rows=keep_rows,
                             keep_every=keep_every)
    keep_spec = pl.BlockSpec((keep_rows, ATT_WIDTH), lambda i, j: (i // keep_every, 0))
    return pl.pallas_call(
        kern,
        grid=(m // bm, Z_WIDTH // INPROJ_BN),
        in_specs=[
            pl.BlockSpec((bm, D_MODEL), lambda i, j: (i, 0)),
            pl.BlockSpec((D_MODEL, INPROJ_BN), lambda i, j: (0, j)),
            pl.BlockSpec((D_MODEL, ALO_PAD), lambda i, j: (0, 0)),
        ],
        out_specs=[
            pl.BlockSpec((bm, INPROJ_BN), lambda i, j: (i, j)),
            pl.BlockSpec((bm, ALO_PAD), lambda i, j: (i, 0)),
            keep_spec,
            keep_spec,
        ],
        out_shape=[
            jax.ShapeDtypeStruct((m, Z_WIDTH), BF16),
            jax.ShapeDtypeStruct((m, ALO_PAD), F32),
            jax.ShapeDtypeStruct((n_keep, ATT_WIDTH), F32),
            jax.ShapeDtypeStruct((n_keep, ATT_WIDTH), F32),
        ],
        scratch_shapes=[pltpu.VMEM((bm, D_MODEL), BF16)],
        compiler_params=_cparams(2),
        name="inproj",
    )(x2d, w_main, w_alo)


def _gla_kernel(q_ref, k_ref, v_ref, alo_ref, gup_ref, gb_ref, s0_ref, o_ref, sfin_ref, st_ref, *,
                rows_per_step, sb, blk):
    t = pl.program_id(1)
    n_sub = sb // blk
    shift = blk.bit_length() - 1
    assert 1 << shift == blk

    @pl.when(t == 0)
    def _():
        for h in range(GLA_HEADS):
            st_ref[h] = s0_ref[0, h].T

    row = lax.broadcasted_iota(jnp.int32, (sb, sb), 0)
    col = lax.broadcasted_iota(jnp.int32, (sb, sb), 1)
    dblk = (row >> shift) - (col >> shift)
    same_blk_causal = (dblk == 0) & (row >= col)
    later_blk = {d: dblk == d for d in range(1, n_sub)}
    tril = jnp.where(same_blk_causal, 1.0, 0.0).astype(BF16)

    def rows_of(blocks):
        return jnp.concatenate([jnp.broadcast_to(r, (blk, GLA_QK)) for r in blocks], axis=0)

    def body(s, carry):
        rows = pl.ds(pl.multiple_of(s * sb, sb), sb)
        x = _dot(alo_ref[rows, :].astype(BF16), gup_ref[...]) + gb_ref[...]
        log_a = (jnp.minimum(x, 0.0) - jnp.log(1.0 + jnp.exp2(jnp.abs(x) * -LOG2_E))) * (1.0 / GLA_TAU)
        hi = _top_bits(log_a)
        rem = log_a - hi
        mid = _top_bits(rem)
        lo = rem - mid
        b = _dot(tril, hi.astype(BF16)) + _dot(tril, mid.astype(BF16)) + _dot(tril, lo.astype(BF16))
        blk_sum = [b[(j + 1) * blk - 1:(j + 1) * blk, :] for j in range(n_sub)]
        cum = [blk_sum[0]]
        for j in range(1, n_sub):
            cum.append(cum[-1] + blk_sum[j])
        total = cum[-1]
        qf = q_ref[rows, :].astype(F32) * (GLA_DK ** -0.5)
        kf = k_ref[rows, :].astype(F32)
        q_tf = qf * jnp.exp(b)
        k_decf = kf * jnp.exp(rows_of(blk_sum) - b)
        q_t = q_tf.astype(BF16)
        k_t = (kf * jnp.exp(-b)).astype(BF16)
        k_dec = k_decf.astype(BF16)
        if n_sub > 1:
            one = jnp.ones((1, GLA_QK), F32)
            q_s = (q_tf * rows_of([one] + [jnp.exp(cum[j - 1]) for j in range(1, n_sub)])).astype(BF16)
            k_s = (k_decf * rows_of([jnp.exp(total - cum[j]) for j in range(n_sub)])).astype(BF16)
            k_far = {1: k_dec}
            zero = jnp.zeros((1, GLA_QK), F32)
            for d in range(2, n_sub):
                scale = [jnp.exp(cum[j + d - 1] - cum[j]) if j + d < n_sub else zero for j in range(n_sub)]
                k_far[d] = (k_decf * rows_of(scale)).astype(BF16)
        else:
            q_s, k_s, k_far = q_t, k_dec, {}
        decay = jnp.exp(total)
        for h in range(GLA_HEADS):
            lk = slice(h * GLA_DK, (h + 1) * GLA_DK)
            lv = slice(h * GLA_DV, (h + 1) * GLA_DV)
            att = jnp.where(same_blk_causal, _dot_nt(q_t[:, lk], k_t[:, lk]), 0.0)
            for d, kd in k_far.items():
                att = jnp.where(later_blk[d], _dot_nt(q_t[:, lk], kd[:, lk]), att)
            vb = v_ref[rows, lv]
            st = st_ref[h]
            o = _dot(att.astype(BF16), vb) + _dot_nt(q_s[:, lk], st.astype(BF16))
            st_ref[h] = st * decay[:, lk] + _dot_tn(vb, k_s[:, lk])
            o_ref[rows, lv] = o.astype(BF16)
        return carry

    n_sb = rows_per_step // sb
    lax.fori_loop(0, n_sb, body, 0, unroll=2 if n_sb % 2 == 0 else 1)

    @pl.when(t == pl.num_programs(1) - 1)
    def _():
        for h in range(GLA_HEADS):
            sfin_ref[0, h] = st_ref[h].T


def _gla(z, alo, gup_pad, gate_b, s0, *, n_seq, seq_len):
    blk = min(CHUNK, seq_len)
    sb = min(GLA_SUPER, seq_len)
    rt = min(GLA_ROWS, seq_len)
    n_t = seq_len // rt
    kern = functools.partial(_gla_kernel, rows_per_step=rt, sb=sb, blk=blk)
    row_block = lambda b, t: b * n_t + t
    return pl.pallas_call(
        kern,
        grid=(n_seq, n_t),
        in_specs=[
            pl.BlockSpec((rt, GLA_QK), lambda b, t: (row_block(b, t), Z_QG // GLA_QK)),
            pl.BlockSpec((rt, GLA_QK), lambda b, t: (row_block(b, t), Z_KG // GLA_QK)),
            pl.BlockSpec((rt, GLA_V), lambda b, t: (row_block(b, t), Z_VG // GLA_V)),
            pl.BlockSpec((rt, ALO_PAD), lambda b, t: (row_block(b, t), 0)),
            pl.BlockSpec((ALO_PAD, GLA_QK), lambda b, t: (0, 0)),
            pl.BlockSpec((1, GLA_QK), lambda b, t: (0, 0)),
            pl.BlockSpec((1, GLA_HEADS, GLA_DK, GLA_DV), lambda b, t: (b, 0, 0, 0)),
        ],
        out_specs=[
            pl.BlockSpec((rt, GLA_V), lambda b, t: (row_block(b, t), 0)),
            pl.BlockSpec((1, GLA_HEADS, GLA_DK, GLA_DV), lambda b, t: (b, 0, 0, 0)),
        ],
        out_shape=[
            jax.ShapeDtypeStruct((n_seq * seq_len, GLA_V), BF16),
            jax.ShapeDtypeStruct((n_seq, GLA_HEADS, GLA_DK, GLA_DV), F32),
        ],
        scratch_shapes=[pltpu.VMEM((GLA_HEADS, GLA_DV, GLA_DK), F32)],
        compiler_params=_cparams(2),
        name="gla",
    )(z, z, z, alo, gup_pad, gate_b, s0)


def _rel_bias_kernel(tab_ref, out_ref, *, n_q, n_k, off, band):
    h = pl.program_id(0)
    k8 = lax.broadcasted_iota(jnp.int32, (8, n_k), 1)
    rel0 = jnp.clip(off - k8, -REL_CLIP, REL_CLIP) + REL_CLIP

    def body(r, acc):
        return jnp.where(rel0 == r, tab_ref[h, r], acc)

    row0 = lax.fori_loop(0, 2 * REL_CLIP + 1, body, jnp.zeros((8, n_k), F32))
    full = jnp.broadcast_to(row0[0:1, :], (n_q, n_k))
    rolled = pltpu.roll(full, 0, 1, stride=1, stride_axis=0)
    q = lax.broadcasted_iota(jnp.int32, (n_q, n_k), 0)
    k = lax.broadcasted_iota(jnp.int32, (n_q, n_k), 1)
    bias = jnp.where(k < q, tab_ref[h, 2 * REL_CLIP], rolled)
    if band:
        dc = (k >> CHUNK_SHIFT) - (q >> CHUNK_SHIFT)
        bias = jnp.where(dc < 0, NEG_INF, jnp.where(dc > BAND_CHUNKS, NEG_INF, bias))
    out_ref[0] = bias


def _rel_bias(table, *, n_q, n_k, off, band):
    assert off >= REL_CLIP and n_k % V7X_LANES == 0
    kern = functools.partial(_rel_bias_kernel, n_q=n_q, n_k=n_k, off=off, band=band)
    return pl.pallas_call(
        kern,
        grid=(ATT_HEADS,),
        in_specs=[pl.BlockSpec(memory_space=pltpu.SMEM)],
        out_specs=pl.BlockSpec((1, n_q, n_k), lambda h: (h, 0, 0)),
        out_shape=jax.ShapeDtypeStruct((ATT_HEADS, n_q, n_k), F32),
        compiler_params=_cparams(1),
        name="rel_bias",
    )(table)


def _band_kernel(q_ref, k_ref, v_ref, bias_ref, o_ref, kp_ref, vp_ref, *, seq_len):
    zeros = jnp.zeros((BAND_PAST, ATT_DH), BF16)
    kp_ref[0:BAND_PAST, :] = zeros
    vp_ref[0:BAND_PAST, :] = zeros
    kp_ref[BAND_PAST:, :] = k_ref[...]
    vp_ref[BAND_PAST:, :] = v_ref[...]
    kk = lax.broadcasted_iota(jnp.int32, (ATT_GROUP, ATT_WINDOW), 1)

    def group(q0, before_start):
        qg = q_ref[pl.ds(q0, ATT_GROUP), :]
        kw = kp_ref[pl.ds(q0, ATT_WINDOW), :]
        vw = vp_ref[pl.ds(q0, ATT_WINDOW), :]
        s = _dot_nt(qg, kw) * (ATT_DH ** -0.5) + bias_ref[0]
        if before_start:
            s = jnp.where(kk >= BAND_PAST - q0, s, NEG_INF)
        m = jnp.max(s, axis=-1, keepdims=True)
        p = jnp.exp(s - m)
        l = jnp.sum(p, axis=-1, keepdims=True)
        o = _dot(p.astype(BF16), vw) / l
        o_ref[pl.ds(q0, ATT_GROUP), :] = o.astype(BF16)

    n_groups = seq_len // ATT_GROUP
    n_head = min(-(-BAND_PAST // ATT_GROUP), n_groups)
    for g in range(n_head):
        group(g * ATT_GROUP, True)

    def body(g, carry):
        group(pl.multiple_of(g * ATT_GROUP, ATT_GROUP), False)
        return carry

    n_body = n_groups - n_head
    lax.fori_loop(n_head, n_groups, body, 0, unroll=3 if n_body % 3 == 0 else 1)


def _band_attention(z, bias, *, n_seq, seq_len):
    qb, kb, vb = Z_QA // ATT_DH, Z_KA // ATT_DH, Z_VA // ATT_DH
    kern = functools.partial(_band_kernel, seq_len=seq_len)
    return pl.pallas_call(
        kern,
        grid=(ATT_HEADS, n_seq),
        in_specs=[
            pl.BlockSpec((seq_len, ATT_DH), lambda h, b: (b, qb + h)),
            pl.BlockSpec((seq_len, ATT_DH), lambda h, b: (b, kb + h)),
            pl.BlockSpec((seq_len, ATT_DH), lambda h, b: (b, vb + h)),
            pl.BlockSpec((1, ATT_GROUP, ATT_WINDOW), lambda h, b: (h, 0, 0)),
        ],
        out_specs=pl.BlockSpec((seq_len, ATT_DH), lambda h, b: (b, h)),
        out_shape=jax.ShapeDtypeStruct((n_seq * seq_len, ATT_WIDTH), BF16),
        scratch_shapes=[pltpu.VMEM((seq_len + BAND_PAST, ATT_DH), BF16),
                        pltpu.VMEM((seq_len + BAND_PAST, ATT_DH), BF16)],
        compiler_params=_cparams(2),
        name="band_attention",
    )(z, z, z, bias)


def _band_step_kernel(q_ref, kn_ref, vn_ref, kc_ref, vc_ref, bias_ref, o_ref, *, n_new, n_past):
    scale = ATT_DH ** -0.5
    for h in range(ATT_HEADS):
        lanes = slice(h * ATT_DH, (h + 1) * ATT_DH)
        qh = q_ref[:, lanes]
        kc = kc_ref[0, pl.ds(h, n_past, stride=ATT_HEADS), :].astype(BF16)
        vc = vc_ref[0, pl.ds(h, n_past, stride=ATT_HEADS), :].astype(BF16)
        s_p = _dot_nt(qh, kc) * scale + bias_ref[h, :, 0:n_past]
        s_n = _dot_nt(qh, kn_ref[:, lanes]) * scale + bias_ref[h, :, n_past:n_past + n_new]
        m = jnp.maximum(jnp.max(s_p, axis=-1, keepdims=True), jnp.max(s_n, axis=-1, keepdims=True))
        p_p = jnp.exp(s_p - m)
        p_n = jnp.exp(s_n - m)
        l = jnp.sum(p_p, axis=-1, keepdims=True) + jnp.sum(p_n, axis=-1, keepdims=True)
        o = (_dot(p_p.astype(BF16), vc) + _dot(p_n.astype(BF16), vn_ref[:, lanes])) / l
        o_ref[:, lanes] = o.astype(BF16)


def _band_attention_step(z, cache_k, cache_v, bias, *, n_seq, n_new):
    n_past = cache_k.shape[1]
    kc = cache_k.reshape(n_seq, n_past * ATT_HEADS, ATT_DH)
    vc = cache_v.reshape(n_seq, n_past * ATT_HEADS, ATT_DH)
    qb, kb, vb = Z_QA // ATT_WIDTH, Z_KA // ATT_WIDTH, Z_VA // ATT_WIDTH
    kern = functools.partial(_band_step_kernel, n_new=n_new, n_past=n_past)
    return pl.pallas_call(
        kern,
        grid=(n_seq,),
        in_specs=[
            pl.BlockSpec((n_new, ATT_WIDTH), lambda b: (b, qb)),
            pl.BlockSpec((n_new, ATT_WIDTH), lambda b: (b, kb)),
            pl.BlockSpec((n_new, ATT_WIDTH), lambda b: (b, vb)),
            pl.BlockSpec((1, n_past * ATT_HEADS, ATT_DH), lambda b: (b, 0, 0)),
            pl.BlockSpec((1, n_past * ATT_HEADS, ATT_DH), lambda b: (b, 0, 0)),
            pl.BlockSpec(bias.shape, lambda b: (0, 0, 0)),
        ],
        out_specs=pl.BlockSpec((n_new, ATT_WIDTH), lambda b: (b, 0)),
        out_shape=jax.ShapeDtypeStruct((n_seq * n_new, ATT_WIDTH), BF16),
        compiler_params=_cparams(1),
        name="band_attention_step",
    )(z, z, z, kc, vc, bias)


def _merge_kernel(og_ref, r_ref, oa_ref, mg_ref, ma_ref, x_ref, ng_ref, wg_ref, wa_ref, wo_ref, mb_ref,
                  g_ref, b_ref, h_ref, *, alpha):
    og = og_ref[...].astype(F32)
    normed = []
    for hd in range(GLA_HEADS):
        oh = og[:, hd * GLA_DV:(hd + 1) * GLA_DV]
        ms = jnp.mean(oh * oh, axis=-1, keepdims=True)
        normed.append(oh * lax.rsqrt(ms + RMS_EPS))
    rf = r_ref[...].astype(F32)
    o_gla = (jnp.concatenate(normed, axis=1) * ng_ref[...]) * (rf * _sigmoid(rf))
    gate_g = _sigmoid(mg_ref[...].astype(F32) + mb_ref[0:1, :])
    gate_a = _sigmoid(ma_ref[...].astype(F32) + mb_ref[1:2, :])
    mixed = (gate_g * _dot(o_gla.astype(BF16), wg_ref[...])
             + gate_a * _dot(oa_ref[...], wa_ref[...]))
    mix = _dot(mixed.astype(BF16), wo_ref[...])
    h_ref[...] = _layer_norm(alpha * x_ref[...] + mix, g_ref[...], b_ref[...])


def _merge(o_gla_raw, o_att, z, x2d, norm_g, w_br_gla, w_br_att, w_out, merge_b, ln_g, ln_b, *, alpha):
    m = x2d.shape[0]
    bm = MERGE_BM
    rgb, mgb, mab = Z_RG // D_MODEL, Z_MG // D_MODEL, Z_MA // D_MODEL
    resident = dict(pipeline_mode=pl.Buffered(1))
    kern = functools.partial(_merge_kernel, alpha=alpha)
    return pl.pallas_call(
        kern,
        grid=(m // bm,),
        in_specs=[
            pl.BlockSpec((bm, GLA_V), lambda i: (i, 0)),
            pl.BlockSpec((bm, GLA_V), lambda i: (i, rgb)),
            pl.BlockSpec((bm, ATT_WIDTH), lambda i: (i, 0)),
            pl.BlockSpec((bm, D_MODEL), lambda i: (i, mgb)),
            pl.BlockSpec((bm, D_MODEL), lambda i: (i, mab)),
            pl.BlockSpec((bm, D_MODEL), lambda i: (i, 0)),
            pl.BlockSpec((1, GLA_V), lambda i: (0, 0)),
            pl.BlockSpec((GLA_V, D_MODEL), lambda i: (0, 0), **resident),
            pl.BlockSpec((ATT_WIDTH, D_MODEL), lambda i: (0, 0), **resident),
            pl.BlockSpec((D_MODEL, D_MODEL), lambda i: (0, 0), **resident),
            pl.BlockSpec((2, D_MODEL), lambda i: (0, 0)),
            pl.BlockSpec((1, D_MODEL), lambda i: (0, 0)),
            pl.BlockSpec((1, D_MODEL), lambda i: (0, 0)),
        ],
        out_specs=pl.BlockSpec((bm, D_MODEL), lambda i: (i, 0)),
        out_shape=jax.ShapeDtypeStruct((m, D_MODEL), F32),
        compiler_params=_cparams(1),
        name="merge",
    )(o_gla_raw, z, o_att, z, z, x2d, norm_g, w_br_gla, w_br_att, w_out, merge_b, ln_g, ln_b)


def _ffn_kernel(h_ref, wu_ref, wg_ref, wd_ref, cw_ref, cb_ref, g_ref, b_ref, e0_ref, e1_ref,
                y_ref, ut_ref, hb_ref, carry_ref, *, alpha, seq_len, bm, whole_u):
    i = pl.program_id(0)
    j = pl.program_id(1)

    @pl.when(j == 0)
    def _():
        hb_ref[...] = h_ref[...].astype(BF16)
        y_ref[...] = jnp.zeros(y_ref.shape, F32)

    if seq_len >= bm:
        @pl.when(i == 0)
        def _():
            carry_ref[j] = jnp.zeros(carry_ref.shape[1:], F32)

    hb = hb_ref[...]
    row = lax.broadcasted_iota(jnp.int32, (bm, FFN_HALF), 0)
    part = None
    for c in range(FFN_BF // FFN_HALF):
        cols = slice(c * FFN_HALF, (c + 1) * FFN_HALF)
        u = _dot(hb, wu_ref[:, cols])
        gate = _dot(hb, wg_ref[:, cols])
        if seq_len >= bm:
            first = lax.rem(i, seq_len // bm) == 0
            p0 = jnp.where(first, e0_ref[0, :, cols], carry_ref[j, 6:7, cols])
            p1 = jnp.where(first, e1_ref[0, :, cols], carry_ref[j, 7:8, cols])
            carry_ref[j, :, cols] = u[bm - 8:bm, :]
            is0 = row == 0
            is1 = row == 1
        else:
            p0 = e0_ref[:, cols]
            p1 = e1_ref[:, cols]
            pos = row & (seq_len - 1)
            is0 = pos == 0
            is1 = pos == 1
        u_m1 = jnp.where(is0, p1, pltpu.roll(u, 1, 0))
        u_m2 = jnp.where(is0, p0, jnp.where(is1, p1, pltpu.roll(u, 2, 0)))
        uc = (cb_ref[:, cols] + u_m2 * cw_ref[0:1, cols] + u_m1 * cw_ref[1:2, cols]
              + u * cw_ref[2:3, cols])
        gelu = 0.5 * uc * (1.0 + jnp.tanh(math.sqrt(2.0 / math.pi) * (uc + 0.044715 * (uc * uc * uc))))
        d = _dot((gelu * gate).astype(BF16), wd_ref[cols, :])
        part = d if part is None else part + d
        if whole_u:
            ut_ref[:, cols] = u
        else:
            ut_ref[0, :, cols] = u[bm - 8:bm, :]
    y_ref[...] += part

    @pl.when(j == pl.num_programs(1) - 1)
    def _():
        y_ref[...] = _layer_norm(alpha * h_ref[...] + y_ref[...], g_ref[...], b_ref[...])


def _ffn(h, w_up_u, w_up_g, w_down, conv_w, conv_b, ln_g, ln_b, conv_prev, *, alpha, n_seq, seq_len):
    m = h.shape[0]
    n_ff = D_FF_PAD // FFN_BF
    bm = FFN_BM
    if seq_len >= bm:
        tiles_per_seq = seq_len // bm
        e0 = conv_prev[:, 0:1, :]
        e1 = conv_prev[:, 1:2, :]
        e_spec = pl.BlockSpec((1, 1, FFN_BF), lambda i, j: (i // tiles_per_seq, 0, j))
        ut_shape = jax.ShapeDtypeStruct((m // bm, 8, D_FF_PAD), F32)
        ut_spec = pl.BlockSpec((1, 8, FFN_BF), lambda i, j: (i, 0, j))
        whole_u = False
    else:
        assert bm % seq_len == 0 and seq_len & (seq_len - 1) == 0 and seq_len >= CONV_W - 1
        e0 = jnp.repeat(conv_prev[:, 0, :], seq_len, axis=0)
        e1 = jnp.repeat(conv_prev[:, 1, :], seq_len, axis=0)
        e_spec = pl.BlockSpec((bm, FFN_BF), lambda i, j: (i, j))
        ut_shape = jax.ShapeDtypeStruct((m, D_FF_PAD), F32)
        ut_spec = pl.BlockSpec((bm, FFN_BF), lambda i, j: (i, j))
        whole_u = True
    kern = functools.partial(_ffn_kernel, alpha=alpha, seq_len=seq_len, bm=bm, whole_u=whole_u)
    return pl.pallas_call(
        kern,
        grid=(m // bm, n_ff),
        in_specs=[
            pl.BlockSpec((bm, D_MODEL), lambda i, j: (i, 0)),
            pl.BlockSpec((D_MODEL, FFN_BF), lambda i, j: (0, j)),
            pl.BlockSpec((D_MODEL, FFN_BF), lambda i, j: (0, j)),
            pl.BlockSpec((FFN_BF, D_MODEL), lambda i, j: (j, 0)),
            pl.BlockSpec((CONV_W, FFN_BF), lambda i, j: (0, j)),
            pl.BlockSpec((1, FFN_BF), lambda i, j: (0, j)),
            pl.BlockSpec((1, D_MODEL), lambda i, j: (0, 0)),
            pl.BlockSpec((1, D_MODEL), lambda i, j: (0, 0)),
            e_spec,
            e_spec,
        ],
        out_specs=[pl.BlockSpec((bm, D_MODEL), lambda i, j: (i, 0)), ut_spec],
        out_shape=[jax.ShapeDtypeStruct((m, D_MODEL), F32), ut_shape],
        scratch_shapes=[pltpu.VMEM((bm, D_MODEL), BF16),
                        pltpu.VMEM((n_ff, 8, FFN_BF), F32)],
        compiler_params=_cparams(2),
        name="ffn",
    )(h, w_up_u, w_up_g, w_down, conv_w, conv_b, ln_g, ln_b, e0, e1)


def _prepare_params(w_in, gla_gate_up, gla_gate_b, gla_norm_g, w_br_gla, w_br_att, w_out,
                    ln1_g, ln1_b, w_ffn_up, ffn_conv_w, ffn_conv_b, w_ffn_down, ln2_g, ln2_b):
    a0 = 2 * GLA_QK + 2 * GLA_V
    a1 = a0 + GLA_GATE_RANK
    att = w_in[:, a1:a1 + 3 * ATT_WIDTH]
    gates = w_in[:, a1 + 3 * ATT_WIDTH:]
    w_main = jnp.concatenate([w_in[:, :a0], gates, att], axis=1).astype(BF16)
    w_alo = jnp.pad(w_in[:, a0:a1], ((0, 0), (0, ALO_PAD - GLA_GATE_RANK))).astype(BF16)
    gup = jnp.pad(gla_gate_up, ((0, ALO_PAD - GLA_GATE_RANK), (0, 0))).astype(BF16)
    ffpad = D_FF_PAD - D_FF
    return dict(
        w_main=w_main, w_alo=w_alo, gup=gup,
        gate_b=gla_gate_b.reshape(1, GLA_QK), norm_g=gla_norm_g.reshape(1, GLA_V),
        w_br_gla=w_br_gla.astype(BF16), w_br_att=w_br_att.astype(BF16), w_out=w_out.astype(BF16),
        ln1_g=ln1_g.reshape(1, D_MODEL), ln1_b=ln1_b.reshape(1, D_MODEL),
        w_up_u=jnp.pad(w_ffn_up[:, :D_FF], ((0, 0), (0, ffpad))).astype(BF16),
        w_up_g=jnp.pad(w_ffn_up[:, D_FF:], ((0, 0), (0, ffpad))).astype(BF16),
        w_down=jnp.pad(w_ffn_down, ((0, ffpad), (0, 0))).astype(BF16),
        conv_w=jnp.pad(ffn_conv_w, ((0, 0), (0, ffpad))),
        conv_b=jnp.pad(ffn_conv_b.reshape(1, D_FF), ((0, 0), (0, ffpad))),
        ln2_g=ln2_g.reshape(1, D_MODEL), ln2_b=ln2_b.reshape(1, D_MODEL),
    )


def _layer(x, past_k, past_v, s0, conv_prev, p, table, merge_b, *, alpha):
    n_seq, seq_len, _ = x.shape
    m = n_seq * seq_len
    x2d = x.reshape(m, D_MODEL)
    z, alo, k_new, v_new = _inproj(x2d, p["w_main"], p["w_alo"], seq_len=seq_len)
    o_gla, s_fin = _gla(z, alo, p["gup"], p["gate_b"], s0, n_seq=n_seq, seq_len=seq_len)
    keep = min(BAND_PAST, seq_len)
    k_new = k_new.reshape(n_seq, keep, ATT_HEADS, ATT_DH)
    v_new = v_new.reshape(n_seq, keep, ATT_HEADS, ATT_DH)
    if past_k is None:
        bias = _rel_bias(table, n_q=ATT_GROUP, n_k=ATT_WINDOW, off=BAND_PAST, band=True)
        o_att = _band_attention(z, bias, n_seq=n_seq, seq_len=seq_len)
    else:
        n_past = past_k.shape[1]
        n_k = ((n_past + seq_len + V7X_LANES - 1) // V7X_LANES) * V7X_LANES
        bias = _rel_bias(table, n_q=seq_len, n_k=n_k, off=n_past, band=False)
        o_att = _band_attention_step(z, past_k, past_v, bias, n_seq=n_seq, n_new=seq_len)
    h = _merge(o_gla, o_att, z, x2d, p["norm_g"], p["w_br_gla"], p["w_br_att"], p["w_out"], merge_b,
               p["ln1_g"], p["ln1_b"], alpha=alpha)
    prev = jnp.pad(conv_prev, ((0, 0), (0, 0), (0, D_FF_PAD - D_FF)))
    y, ut = _ffn(h, p["w_up_u"], p["w_up_g"], p["w_down"], p["conv_w"], p["conv_b"],
                 p["ln2_g"], p["ln2_b"], prev, alpha=alpha, n_seq=n_seq, seq_len=seq_len)
    if seq_len >= FFN_BM:
        tiles_per_seq = seq_len // FFN_BM
        conv_new = ut.reshape(n_seq, tiles_per_seq, 8, D_FF_PAD)[:, -1, 8 - (CONV_W - 1):, :D_FF]
    else:
        conv_new = ut.reshape(n_seq, seq_len, D_FF_PAD)[:, seq_len - (CONV_W - 1):, :D_FF]
    return (y.reshape(n_seq, seq_len, D_MODEL), k_new, v_new, s_fin, conv_new)


def kernel(x_prompt, x_sample, cache_att_k, cache_att_v, state_gla, state_ffn_conv, w_in, gla_gate_up,
           gla_gate_b, gla_norm_g, att_rel_bias, merge_b, w_br_gla, w_br_att, w_out, ln1_g, ln1_b,
           w_ffn_up, ffn_conv_w, ffn_conv_b, w_ffn_down, ln2_g, ln2_b):
    depth = w_in.shape[0]
    alpha = (2.0 * depth) ** 0.25
    xp, xs = x_prompt, x_sample
    n_p = xp.shape[0]
    outs_p, outs_s = [], []
    for l in range(depth):
        p = _prepare_params(w_in[l], gla_gate_up[l], gla_gate_b[l], gla_norm_g[l], w_br_gla[l],
                            w_br_att[l], w_out[l], ln1_g[l], ln1_b[l], w_ffn_up[l], ffn_conv_w[l],
                            ffn_conv_b[l], w_ffn_down[l], ln2_g[l], ln2_b[l])
        s0_p = jnp.zeros((n_p, GLA_HEADS, GLA_DK, GLA_DV), F32)
        conv0_p = jnp.zeros((n_p, CONV_W - 1, D_FF), F32)
        xp, *rest_p = _layer(xp, None, None, s0_p, conv0_p, p, att_rel_bias[l], merge_b[l], alpha=alpha)
        outs_p.append(rest_p)
        xs, *rest_s = _layer(xs, cache_att_k[l], cache_att_v[l], state_gla[l], state_ffn_conv[l], p,
                             att_rel_bias[l], merge_b[l], alpha=alpha)
        outs_s.append(rest_s)
    stack = lambda outs, idx: jnp.stack([o[idx] for o in outs])
    return (xp, xs,
            stack(outs_p, 0), stack(outs_p, 1), stack(outs_p, 2), stack(outs_p, 3),
            stack(outs_s, 0), stack(outs_s, 1), stack(outs_s, 2), stack(outs_s, 3))
```

```python
import functools
import math

import jax
import jax.numpy as jnp
from jax import lax
from jax.experimental import pallas as pl
from jax.experimental.pallas import tpu as pltpu

F32 = jnp.float32
BF16 = jnp.bfloat16

D_MODEL = 2048
CHUNK = 64
CHUNK_SHIFT = CHUNK.bit_length() - 1
assert 1 << CHUNK_SHIFT == CHUNK
GLA_HEADS = 4
GLA_QK = D_MODEL // 2
GLA_V = D_MODEL
GLA_DK = GLA_QK // GLA_HEADS
GLA_DV = GLA_V // GLA_HEADS
GLA_GATE_RANK = 16
GLA_TAU = 16.0
ATT_HEADS = 8
ATT_DH = 128
ATT_WIDTH = ATT_HEADS * ATT_DH
BAND_CHUNKS = 8
BAND_PAST = BAND_CHUNKS * CHUNK
REL_CLIP = 128
D_FF = 5504
CONV_W = 3
LN_EPS = 1e-5
RMS_EPS = 1e-6
NEG_INF = -1e30

V7X_LANES = 128
V7X_VMEM_BYTES = 64 * 1024 * 1024
VMEM_LIMIT = V7X_VMEM_BYTES - 4 * 1024 * 1024

Z_QG = 0
Z_KG = Z_QG + GLA_QK
Z_VG = Z_KG + GLA_QK
Z_RG = Z_VG + GLA_V
Z_MG = Z_RG + GLA_V
Z_MA = Z_MG + D_MODEL
Z_QA = Z_MA + D_MODEL
Z_KA = Z_QA + ATT_WIDTH
Z_VA = Z_KA + ATT_WIDTH
Z_WIDTH = Z_VA + ATT_WIDTH
ALO_PAD = V7X_LANES

INPROJ_BM = 1024
INPROJ_BN = 1024
MERGE_BM = 256
GLA_ROWS = 512
GLA_SUPER = 4 * CHUNK
FFN_BM = 1024
FFN_BM_STREAMS = 512
FFN_BF = 512
FFN_HALF = 512
FFN_ROWS = 128
D_FF_PAD = ((D_FF + FFN_BF - 1) // FFN_BF) * FFN_BF
ATT_GROUP = 4 * CHUNK
ATT_WINDOW = ATT_GROUP + BAND_PAST


def _cparams(n_axes):
    return pltpu.CompilerParams(
        dimension_semantics=("arbitrary",) * n_axes, vmem_limit_bytes=VMEM_LIMIT)


def _dot(a, b):
    return jnp.dot(a, b, preferred_element_type=F32)


def _dot_nt(a, b):
    return lax.dot_general(a, b, (((1,), (1,)), ((), ())), preferred_element_type=F32)


def _dot_tn(a, b):
    return lax.dot_general(a, b, (((0,), (0,)), ((), ())), preferred_element_type=F32)


LOG2_E = 1.4426950408889634


def _top_bits(x):
    bits = lax.bitcast_convert_type(x, jnp.uint32) & jnp.uint32(0xFFFF0000)
    return lax.bitcast_convert_type(bits, F32)


def _sigmoid(x):
    return 1.0 / (1.0 + jnp.exp(-x))


def _layer_norm(x, g, b):
    mu = jnp.mean(x, axis=-1, keepdims=True)
    xc = x - mu
    var = jnp.mean(xc * xc, axis=-1, keepdims=True)
    return xc * lax.rsqrt(var + LN_EPS) * g + b


def _inproj_kernel(x_ref, w_ref, walo_ref, z_ref, alo_ref, k32_ref, v32_ref, xb_ref, *,
                   keep_from, keep_rows, keep_every):
    i = pl.program_id(0)
    j = pl.program_id(1)

    @pl.when(j == 0)
    def _():
        xb_ref[...] = x_ref[...].astype(BF16)
        alo_ref[...] = _dot(xb_ref[...], walo_ref[...])

    res = _dot(xb_ref[...], w_ref[...])
    z_ref[...] = res.astype(BF16)

    keep_tile = lax.rem(i, keep_every) == keep_every - 1

    @pl.when(keep_tile & (j == Z_KA // INPROJ_BN))
    def _():
        k32_ref[...] = res[keep_from:keep_from + keep_rows, :]

    @pl.when(keep_tile & (j == Z_VA // INPROJ_BN))
    def _():
        v32_ref[...] = res[keep_from:keep_from + keep_rows, :]


def _inproj(x2d, w_main, w_alo, *, seq_len):
    assert INPROJ_BN == ATT_WIDTH and Z_KA % INPROJ_BN == 0 and Z_VA % INPROJ_BN == 0
    m = x2d.shape[0]
    bm = min(INPROJ_BM, m)
    keep = min(BAND_PAST, seq_len)
    if seq_len >= bm:
        assert seq_len % bm == 0 and keep <= bm
        keep_from, keep_rows, keep_every = bm - keep, keep, seq_len // bm
    else:
        assert keep == seq_len and bm % seq_len == 0
        keep_from, keep_rows, keep_every = 0, bm, 1
    n_keep = m // bm // keep_every * keep_rows
    kern = functools.partial(_inproj_kernel, keep_from=keep_from, keep_rows=keep_rows,
                             keep_every=keep_every)
    keep_spec = pl.BlockSpec((keep_rows, ATT_WIDTH), lambda i, j: (i // keep_every, 0))
    return pl.pallas_call(
        kern,
        grid=(m // bm, Z_WIDTH // INPROJ_BN),
        in_specs=[
            pl.BlockSpec((bm, D_MODEL), lambda i, j: (i, 0)),
            pl.BlockSpec((D_MODEL, INPROJ_BN), lambda i, j: (0, j)),
            pl.BlockSpec((D_MODEL, ALO_PAD), lambda i, j: (0, 0)),
        ],
        out_specs=[
            pl.BlockSpec((bm, INPROJ_BN), lambda i, j: (i, j)),
            pl.BlockSpec((bm, ALO_PAD), lambda i, j: (i, 0)),
            keep_spec,
            keep_spec,
        ],
        out_shape=[
            jax.ShapeDtypeStruct((m, Z_WIDTH), BF16),
            jax.ShapeDtypeStruct((m, ALO_PAD), F32),
            jax.ShapeDtypeStruct((n_keep, ATT_WIDTH), F32),
            jax.ShapeDtypeStruct((n_keep, ATT_WIDTH), F32),
        ],
        scratch_shapes=[pltpu.VMEM((bm, D_MODEL), BF16)],
        compiler_params=_cparams(2),
        name="inproj",
    )(x2d, w_main, w_alo)


def _gla_kernel(q_ref, k_ref, v_ref, alo_ref, gup_ref, gb_ref, s0_ref, o_ref, sfin_ref, st_ref, *,
                rows_per_step, sb, blk):
    t = pl.program_id(1)
    n_sub = sb // blk
    shift = blk.bit_length() - 1
    assert 1 << shift == blk

    @pl.when(t == 0)
    def _():
        for h in range(GLA_HEADS):
            st_ref[h] = s0_ref[0, h].T

    row = lax.broadcasted_iota(jnp.int32, (sb, sb), 0)
    col = lax.broadcasted_iota(jnp.int32, (sb, sb), 1)
    dblk = (row >> shift) - (col >> shift)
    same_blk_causal = (dblk == 0) & (row >= col)
    later_blk = {d: dblk == d for d in range(1, n_sub)}
    tril = jnp.where(same_blk_causal, 1.0, 0.0).astype(BF16)

    def rows_of(blocks):
        return jnp.concatenate([jnp.broadcast_to(r, (blk, GLA_QK)) for r in blocks], axis=0)

    def body(s, carry):
        rows = pl.ds(pl.multiple_of(s * sb, sb), sb)
        x = _dot(alo_ref[rows, :].astype(BF16), gup_ref[...]) + gb_ref[...]
        log_a = (jnp.minimum(x, 0.0) - jnp.log(1.0 + jnp.exp2(jnp.abs(x) * -LOG2_E))) * (1.0 / GLA_TAU)
        hi = _top_bits(log_a)
        rem = log_a - hi
        mid = _top_bits(rem)
        lo = rem - mid
        b = _dot(tril, hi.astype(BF16)) + _dot(tril, mid.astype(BF16)) + _dot(tril, lo.astype(BF16))
        blk_sum = [b[(j + 1) * blk - 1:(j + 1) * blk, :] for j in range(n_sub)]
        cum = [blk_sum[0]]
        for j in range(1, n_sub):
            cum.append(cum[-1] + blk_sum[j])
        total = cum[-1]
        qf = q_ref[rows, :].astype(F32) * (GLA_DK ** -0.5)
        kf = k_ref[rows, :].astype(F32)
        q_tf = qf * jnp.exp(b)
        k_decf = kf * jnp.exp(rows_of(blk_sum) - b)
        q_t = q_tf.astype(BF16)
        k_t = (kf * jnp.exp(-b)).astype(BF16)
        k_dec = k_decf.astype(BF16)
        if n_sub > 1:
            one = jnp.ones((1, GLA_QK), F32)
            q_s = (q_tf * rows_of([one] + [jnp.exp(cum[j - 1]) for j in range(1, n_sub)])).astype(BF16)
            k_s = (k_decf * rows_of([jnp.exp(total - cum[j]) for j in range(n_sub)])).astype(BF16)
            k_far = {1: k_dec}
            zero = jnp.zeros((1, GLA_QK), F32)
            for d in range(2, n_sub):
                scale = [jnp.exp(cum[j + d - 1] - cum[j]) if j + d < n_sub else zero for j in range(n_sub)]
                k_far[d] = (k_decf * rows_of(scale)).astype(BF16)
        else:
            q_s, k_s, k_far = q_t, k_dec, {}
        decay = jnp.exp(total)
        for h in range(GLA_HEADS):
            lk = slice(h * GLA_DK, (h + 1) * GLA_DK)
            lv = slice(h * GLA_DV, (h + 1) * GLA_DV)
            att = jnp.where(same_blk_causal, _dot_nt(q_t[:, lk], k_t[:, lk]), 0.0)
            for d, kd in k_far.items():
                att = jnp.where(later_blk[d], _dot_nt(q_t[:, lk], kd[:, lk]), att)
            vb = v_ref[rows, lv]
            st = st_ref[h]
            o = _dot(att.astype(BF16), vb) + _dot_nt(q_s[:, lk], st.astype(BF16))
            st_ref[h] = st * decay[:, lk] + _dot_tn(vb, k_s[:, lk])
            o_ref[rows, lv] = o.astype(BF16)
        return carry

    n_sb = rows_per_step // sb
    lax.fori_loop(0, n_sb, body, 0, unroll=2 if n_sb % 2 == 0 else 1)

    @pl.when(t == pl.num_programs(1) - 1)
    def _():
        for h in range(GLA_HEADS):
            sfin_ref[0, h] = st_ref[h].T


def _gla(z, alo, gup_pad, gate_b, s0, *, n_seq, seq_len):
    blk = min(CHUNK, seq_len)
    sb = min(GLA_SUPER, seq_len)
    rt = min(GLA_ROWS, seq_len)
    n_t = seq_len // rt
    kern = functools.partial(_gla_kernel, rows_per_step=rt, sb=sb, blk=blk)
    row_block = lambda b, t: b * n_t + t
    return pl.pallas_call(
        kern,
        grid=(n_seq, n_t),
        in_specs=[
            pl.BlockSpec((rt, GLA_QK), lambda b, t: (row_block(b, t), Z_QG // GLA_QK)),
            pl.BlockSpec((rt, GLA_QK), lambda b, t: (row_block(b, t), Z_KG // GLA_QK)),
            pl.BlockSpec((rt, GLA_V), lambda b, t: (row_block(b, t), Z_VG // GLA_V)),
            pl.BlockSpec((rt, ALO_PAD), lambda b, t: (row_block(b, t), 0)),
            pl.BlockSpec((ALO_PAD, GLA_QK), lambda b, t: (0, 0)),
            pl.BlockSpec((1, GLA_QK), lambda b, t: (0, 0)),
            pl.BlockSpec((1, GLA_HEADS, GLA_DK, GLA_DV), lambda b, t: (b, 0, 0, 0)),
        ],
        out_specs=[
            pl.BlockSpec((rt, GLA_V), lambda b, t: (row_block(b, t), 0)),
            pl.BlockSpec((1, GLA_HEADS, GLA_DK, GLA_DV), lambda b, t: (b, 0, 0, 0)),
        ],
        out_shape=[
            jax.ShapeDtypeStruct((n_seq * seq_len, GLA_V), BF16),
            jax.ShapeDtypeStruct((n_seq, GLA_HEADS, GLA_DK, GLA_DV), F32),
        ],
        scratch_shapes=[pltpu.VMEM((GLA_HEADS, GLA_DV, GLA_DK), F32)],
        compiler_params=_cparams(2),
        name="gla",
    )(z, z, z, alo, gup_pad, gate_b, s0)


def _rel_bias_kernel(tab_ref, out_ref, *, n_q, n_k, off, band):
    h = pl.program_id(0)
    k8 = lax.broadcasted_iota(jnp.int32, (8, n_k), 1)
    rel0 = jnp.clip(off - k8, -REL_CLIP, REL_CLIP) + REL_CLIP

    def body(r, acc):
        return jnp.where(rel0 == r, tab_ref[h, r], acc)

    row0 = lax.fori_loop(0, 2 * REL_CLIP + 1, body, jnp.zeros((8, n_k), F32))
    full = jnp.broadcast_to(row0[0:1, :], (n_q, n_k))
    rolled = pltpu.roll(full, 0, 1, stride=1, stride_axis=0)
    q = lax.broadcasted_iota(jnp.int32, (n_q, n_k), 0)
    k = lax.broadcasted_iota(jnp.int32, (n_q, n_k), 1)
    bias = jnp.where(k < q, tab_ref[h, 2 * REL_CLIP], rolled)
    if band:
        dc = (k >> CHUNK_SHIFT) - (q >> CHUNK_SHIFT)
        bias = jnp.where(dc < 0, NEG_INF, jnp.where(dc > BAND_CHUNKS, NEG_INF, bias))
    out_ref[0] = bias


def _rel_bias(table, *, n_q, n_k, off, band):
    assert off >= REL_CLIP and n_k % V7X_LANES == 0
    kern = functools.partial(_rel_bias_kernel, n_q=n_q, n_k=n_k, off=off, band=band)
    return pl.pallas_call(
        kern,
        grid=(ATT_HEADS,),
        in_specs=[pl.BlockSpec(memory_space=pltpu.SMEM)],
        out_specs=pl.BlockSpec((1, n_q, n_k), lambda h: (h, 0, 0)),
        out_shape=jax.ShapeDtypeStruct((ATT_HEADS, n_q, n_k), F32),
        compiler_params=_cparams(1),
        name="rel_bias",
    )(table)


def _band_kernel(q_ref, k_ref, v_ref, bias_ref, o_ref, kp_ref, vp_ref, *, seq_len):
    zeros = jnp.zeros((BAND_PAST, ATT_DH), BF16)
    kp_ref[0:BAND_PAST, :] = zeros
    vp_ref[0:BAND_PAST, :] = zeros
    kp_ref[BAND_PAST:, :] = k_ref[...]
    vp_ref[BAND_PAST:, :] = v_ref[...]
    kk = lax.broadcasted_iota(jnp.int32, (ATT_GROUP, ATT_WINDOW), 1)

    def group(q0, before_start):
        qg = q_ref[pl.ds(q0, ATT_GROUP), :]
        kw = kp_ref[pl.ds(q0, ATT_WINDOW), :]
        vw = vp_ref[pl.ds(q0, ATT_WINDOW), :]
        s = _dot_nt(qg, kw) * (ATT_DH ** -0.5) + bias_ref[0]
        if before_start:
            s = jnp.where(kk >= BAND_PAST - q0, s, NEG_INF)
        m = jnp.max(s, axis=-1, keepdims=True)
        p = jnp.exp(s - m)
        l = jnp.sum(p, axis=-1, keepdims=True)
        o = _dot(p.astype(BF16), vw) / l
        o_ref[pl.ds(q0, ATT_GROUP), :] = o.astype(BF16)

    n_groups = seq_len // ATT_GROUP
    n_head = min(-(-BAND_PAST // ATT_GROUP), n_groups)
    for g in range(n_head):
        group(g * ATT_GROUP, True)

    def body(g, carry):
        group(pl.multiple_of(g * ATT_GROUP, ATT_GROUP), False)
        return carry

    n_body = n_groups - n_head
    lax.fori_loop(n_head, n_groups, body, 0, unroll=3 if n_body % 3 == 0 else 1)


def _band_attention(z, bias, *, n_seq, seq_len):
    qb, kb, vb = Z_QA // ATT_DH, Z_KA // ATT_DH, Z_VA // ATT_DH
    kern = functools.partial(_band_kernel, seq_len=seq_len)
    return pl.pallas_call(
        kern,
        grid=(ATT_HEADS, n_seq),
        in_specs=[
            pl.BlockSpec((seq_len, ATT_DH), lambda h, b: (b, qb + h)),
            pl.BlockSpec((seq_len, ATT_DH), lambda h, b: (b, kb + h)),
            pl.BlockSpec((seq_len, ATT_DH), lambda h, b: (b, vb + h)),
            pl.BlockSpec((1, ATT_GROUP, ATT_WINDOW), lambda h, b: (h, 0, 0)),
        ],
        out_specs=pl.BlockSpec((seq_len, ATT_DH), lambda h, b: (b, h)),
        out_shape=jax.ShapeDtypeStruct((n_seq * seq_len, ATT_WIDTH), BF16),
        scratch_shapes=[pltpu.VMEM((seq_len + BAND_PAST, ATT_DH), BF16),
                        pltpu.VMEM((seq_len + BAND_PAST, ATT_DH), BF16)],
        compiler_params=_cparams(2),
        name="band_attention",
    )(z, z, z, bias)


def _band_step_kernel(q_ref, kn_ref, vn_ref, kc_ref, vc_ref, bias_ref, o_ref, *, n_new, n_past):
    scale = ATT_DH ** -0.5
    for h in range(ATT_HEADS):
        lanes = slice(h * ATT_DH, (h + 1) * ATT_DH)
        qh = q_ref[:, lanes]
        kc = kc_ref[0, pl.ds(h, n_past, stride=ATT_HEADS), :].astype(BF16)
        vc = vc_ref[0, pl.ds(h, n_past, stride=ATT_HEADS), :].astype(BF16)
        s_p = _dot_nt(qh, kc) * scale + bias_ref[h, :, 0:n_past]
        s_n = _dot_nt(qh, kn_ref[:, lanes]) * scale + bias_ref[h, :, n_past:n_past + n_new]
        m = jnp.maximum(jnp.max(s_p, axis=-1, keepdims=True), jnp.max(s_n, axis=-1, keepdims=True))
        p_p = jnp.exp(s_p - m)
        p_n = jnp.exp(s_n - m)
        l = jnp.sum(p_p, axis=-1, keepdims=True) + jnp.sum(p_n, axis=-1, keepdims=True)
        o = (_dot(p_p.astype(BF16), vc) + _dot(p_n.astype(BF16), vn_ref[:, lanes])) / l
        o_ref[:, lanes] = o.astype(BF16)


def _band_attention_step(z, cache_k, cache_v, bias, *, n_seq, n_new):
    n_past = cache_k.shape[1]
    kc = cache_k.reshape(n_seq, n_past * ATT_HEADS, ATT_DH)
    vc = cache_v.reshape(n_seq, n_past * ATT_HEADS, ATT_DH)
    qb, kb, vb = Z_QA // ATT_WIDTH, Z_KA // ATT_WIDTH, Z_VA // ATT_WIDTH
    kern = functools.partial(_band_step_kernel, n_new=n_new, n_past=n_past)
    return pl.pallas_call(
        kern,
        grid=(n_seq,),
        in_specs=[
            pl.BlockSpec((n_new, ATT_WIDTH), lambda b: (b, qb)),
            pl.BlockSpec((n_new, ATT_WIDTH), lambda b: (b, kb)),
            pl.BlockSpec((n_new, ATT_WIDTH), lambda b: (b, vb)),
            pl.BlockSpec((1, n_past * ATT_HEADS, ATT_DH), lambda b: (b, 0, 0)),
            pl.BlockSpec((1, n_past * ATT_HEADS, ATT_DH), lambda b: (b, 0, 0)),
            pl.BlockSpec(bias.shape, lambda b: (0, 0, 0)),
        ],
        out_specs=pl.BlockSpec((n_new, ATT_WIDTH), lambda b: (b, 0)),
        out_shape=jax.ShapeDtypeStruct((n_seq * n_new, ATT_WIDTH), BF16),
        compiler_params=_cparams(1),
        name="band_attention_step",
    )(z, z, z, kc, vc, bias)


def _merge_kernel(og_ref, r_ref, oa_ref, mg_ref, ma_ref, x_ref, ng_ref, wg_ref, wa_ref, wo_ref, mb_ref,
                  g_ref, b_ref, h_ref, *, alpha):
    og = og_ref[...].astype(F32)
    normed = []
    for hd in range(GLA_HEADS):
        oh = og[:, hd * GLA_DV:(hd + 1) * GLA_DV]
        ms = jnp.mean(oh * oh, axis=-1, keepdims=True)
        normed.append(oh * lax.rsqrt(ms + RMS_EPS))
    rf = r_ref[...].astype(F32)
    o_gla = (jnp.concatenate(normed, axis=1) * ng_ref[...]) * (rf * _sigmoid(rf))
    gate_g = _sigmoid(mg_ref[...].astype(F32) + mb_ref[0:1, :])
    gate_a = _sigmoid(ma_ref[...].astype(F32) + mb_ref[1:2, :])
    mixed = (gate_g * _dot(o_gla.astype(BF16), wg_ref[...])
             + gate_a * _dot(oa_ref[...], wa_ref[...]))
    mix = _dot(mixed.astype(BF16), wo_ref[...])
    h_ref[...] = _layer_norm(alpha * x_ref[...] + mix, g_ref[...], b_ref[...])


def _merge(o_gla_raw, o_att, z, x2d, norm_g, w_br_gla, w_br_att, w_out, merge_b, ln_g, ln_b, *, alpha):
    m = x2d.shape[0]
    bm = MERGE_BM
    rgb, mgb, mab = Z_RG // D_MODEL, Z_MG // D_MODEL, Z_MA // D_MODEL
    resident = dict(pipeline_mode=pl.Buffered(1))
    kern = functools.partial(_merge_kernel, alpha=alpha)
    return pl.pallas_call(
        kern,
        grid=(m // bm,),
        in_specs=[
            pl.BlockSpec((bm, GLA_V), lambda i: (i, 0)),
            pl.BlockSpec((bm, GLA_V), lambda i: (i, rgb)),
            pl.BlockSpec((bm, ATT_WIDTH), lambda i: (i, 0)),
            pl.BlockSpec((bm, D_MODEL), lambda i: (i, mgb)),
            pl.BlockSpec((bm, D_MODEL), lambda i: (i, mab)),
            pl.BlockSpec((bm, D_MODEL), lambda i: (i, 0)),
            pl.BlockSpec((1, GLA_V), lambda i: (0, 0)),
            pl.BlockSpec((GLA_V, D_MODEL), lambda i: (0, 0), **resident),
            pl.BlockSpec((ATT_WIDTH, D_MODEL), lambda i: (0, 0), **resident),
            pl.BlockSpec((D_MODEL, D_MODEL), lambda i: (0, 0), **resident),
            pl.BlockSpec((2, D_MODEL), lambda i: (0, 0)),
            pl.BlockSpec((1, D_MODEL), lambda i: (0, 0)),
            pl.BlockSpec((1, D_MODEL), lambda i: (0, 0)),
        ],
        out_specs=pl.BlockSpec((bm, D_MODEL), lambda i: (i, 0)),
        out_shape=jax.ShapeDtypeStruct((m, D_MODEL), F32),
        compiler_params=_cparams(1),
        name="merge",
    )(o_gla_raw, z, o_att, z, z, x2d, norm_g, w_br_gla, w_br_att, w_out, merge_b, ln_g, ln_b)


def _ffn_kernel(h_ref, wu_ref, wg_ref, wd_ref, cw_ref, cb_ref, g_ref, b_ref, e0_ref, e1_ref,
                y_ref, ut_ref, hb_ref, carry_ref, *, alpha, seq_len, bm, whole_u):
    i = pl.program_id(0)
    j = pl.program_id(1)

    @pl.when(j == 0)
    def _():
        hb_ref[...] = h_ref[...].astype(BF16)
        y_ref[...] = jnp.zeros(y_ref.shape, F32)

    if seq_len >= bm:
        @pl.when(i == 0)
        def _():
            carry_ref[j] = jnp.zeros(carry_ref.shape[1:], F32)

    n_col = FFN_BF // FFN_HALF
    row = lax.broadcasted_iota(jnp.int32, (FFN_ROWS, FFN_HALF), 0)
    in_stream = seq_len >= bm
    tails = [None] * n_col
    for r in range(bm // FFN_ROWS):
        rows = slice(r * FFN_ROWS, (r + 1) * FFN_ROWS)
        hb = hb_ref[rows, :]
        part = None
        for c in range(n_col):
            cols = slice(c * FFN_HALF, (c + 1) * FFN_HALF)
            u = _dot(hb, wu_ref[:, cols])
            gate = _dot(hb, wg_ref[:, cols])
            if in_stream:
                if r == 0:
                    first = lax.rem(i, seq_len // bm) == 0
                    p0 = jnp.where(first, e0_ref[0, :, cols], carry_ref[j, 6:7, cols])
                    p1 = jnp.where(first, e1_ref[0, :, cols], carry_ref[j, 7:8, cols])
                else:
                    p0, p1 = tails[c][6:7, :], tails[c][7:8, :]
                tails[c] = u[FFN_ROWS - 8:FFN_ROWS, :]
                is0 = row == 0
                is1 = row == 1
            else:
                p0 = e0_ref[rows, cols]
                p1 = e1_ref[rows, cols]
                pos = row & (seq_len - 1)
                is0 = pos == 0
                is1 = pos == 1
            u_m1 = jnp.where(is0, p1, pltpu.roll(u, 1, 0))
            u_m2 = jnp.where(is0, p0, jnp.where(is1, p1, pltpu.roll(u, 2, 0)))
            uc = (cb_ref[:, cols] + u_m2 * cw_ref[0:1, cols] + u_m1 * cw_ref[1:2, cols]
                  + u * cw_ref[2:3, cols])
            gelu = 0.5 * uc * (1.0 + jnp.tanh(math.sqrt(2.0 / math.pi) * (uc + 0.044715 * (uc * uc * uc))))
            d = _dot((gelu * gate).astype(BF16), wd_ref[cols, :])
            part = d if part is None else part + d
            if whole_u:
                ut_ref[rows, cols] = u
        y_ref[rows, :] += part
    if in_stream:
        for c in range(n_col):
            cols = slice(c * FFN_HALF, (c + 1) * FFN_HALF)
            carry_ref[j, :, cols] = tails[c]
            ut_ref[0, :, cols] = tails[c]

    @pl.when(j == pl.num_programs(1) - 1)
    def _():
        y_ref[...] = _layer_norm(alpha * h_ref[...] + y_ref[...], g_ref[...], b_ref[...])


def _ffn(h, w_up_u, w_up_g, w_down, conv_w, conv_b, ln_g, ln_b, conv_prev, *, alpha, n_seq, seq_len):
    m = h.shape[0]
    n_ff = D_FF_PAD // FFN_BF
    bm = FFN_BM if seq_len >= FFN_BM else FFN_BM_STREAMS
    if seq_len >= bm:
        tiles_per_seq = seq_len // bm
        e0 = conv_prev[:, 0:1, :]
        e1 = conv_prev[:, 1:2, :]
        e_spec = pl.BlockSpec((1, 1, FFN_BF), lambda i, j: (i // tiles_per_seq, 0, j))
        ut_shape = jax.ShapeDtypeStruct((m // bm, 8, D_FF_PAD), F32)
        ut_spec = pl.BlockSpec((1, 8, FFN_BF), lambda i, j: (i, 0, j))
        whole_u = False
    else:
        assert bm % seq_len == 0 and seq_len & (seq_len - 1) == 0 and seq_len >= CONV_W - 1
        e0 = jnp.repeat(conv_prev[:, 0, :], seq_len, axis=0)
        e1 = jnp.repeat(conv_prev[:, 1, :], seq_len, axis=0)
        e_spec = pl.BlockSpec((bm, FFN_BF), lambda i, j: (i, j))
        ut_shape = jax.ShapeDtypeStruct((m, D_FF_PAD), F32)
        ut_spec = pl.BlockSpec((bm, FFN_BF), lambda i, j: (i, j))
        whole_u = True
    kern = functools.partial(_ffn_kernel, alpha=alpha, seq_len=seq_len, bm=bm, whole_u=whole_u)
    return pl.pallas_call(
        kern,
        grid=(m // bm, n_ff),
        in_specs=[
            pl.BlockSpec((bm, D_MODEL), lambda i, j: (i, 0)),
            pl.BlockSpec((D_MODEL, FFN_BF), lambda i, j: (0, j)),
            pl.BlockSpec((D_MODEL, FFN_BF), lambda i, j: (0, j)),
            pl.BlockSpec((FFN_BF, D_MODEL), lambda i, j: (j, 0)),
            pl.BlockSpec((CONV_W, FFN_BF), lambda i, j: (0, j)),
            pl.BlockSpec((1, FFN_BF), lambda i, j: (0, j)),
            pl.BlockSpec((1, D_MODEL), lambda i, j: (0, 0)),
            pl.BlockSpec((1, D_MODEL), lambda i, j: (0, 0)),
            e_spec,
            e_spec,
        ],
        out_specs=[pl.BlockSpec((bm, D_MODEL), lambda i, j: (i, 0)), ut_spec],
        out_shape=[jax.ShapeDtypeStruct((m, D_MODEL), F32), ut_shape],
        scratch_shapes=[pltpu.VMEM((bm, D_MODEL), BF16),
                        pltpu.VMEM((n_ff, 8, FFN_BF), F32)],
        compiler_params=_cparams(2),
        name="ffn",
    )(h, w_up_u, w_up_g, w_down, conv_w, conv_b, ln_g, ln_b, e0, e1)


def _prepare_params(w_in, gla_gate_up, gla_gate_b, gla_norm_g, w_br_gla, w_br_att, w_out,
                    ln1_g, ln1_b, w_ffn_up, ffn_conv_w, ffn_conv_b, w_ffn_down, ln2_g, ln2_b):
    a0 = 2 * GLA_QK + 2 * GLA_V
    a1 = a0 + GLA_GATE_RANK
    att = w_in[:, a1:a1 + 3 * ATT_WIDTH]
    gates = w_in[:, a1 + 3 * ATT_WIDTH:]
    w_main = jnp.concatenate([w_in[:, :a0], gates, att], axis=1).astype(BF16)
    w_alo = jnp.pad(w_in[:, a0:a1], ((0, 0), (0, ALO_PAD - GLA_GATE_RANK))).astype(BF16)
    gup = jnp.pad(gla_gate_up, ((0, ALO_PAD - GLA_GATE_RANK), (0, 0))).astype(BF16)
    ffpad = D_FF_PAD - D_FF
    return dict(
        w_main=w_main, w_alo=w_alo, gup=gup,
        gate_b=gla_gate_b.reshape(1, GLA_QK), norm_g=gla_norm_g.reshape(1, GLA_V),
        w_br_gla=w_br_gla.astype(BF16), w_br_att=w_br_att.astype(BF16), w_out=w_out.astype(BF16),
        ln1_g=ln1_g.reshape(1, D_MODEL), ln1_b=ln1_b.reshape(1, D_MODEL),
        w_up_u=jnp.pad(w_ffn_up[:, :D_FF], ((0, 0), (0, ffpad))).astype(BF16),
        w_up_g=jnp.pad(w_ffn_up[:, D_FF:], ((0, 0), (0, ffpad))).astype(BF16),
        w_down=jnp.pad(w_ffn_down, ((0, ffpad), (0, 0))).astype(BF16),
        conv_w=jnp.pad(ffn_conv_w, ((0, 0), (0, ffpad))),
        conv_b=jnp.pad(ffn_conv_b.reshape(1, D_FF), ((0, 0), (0, ffpad))),
        ln2_g=ln2_g.reshape(1, D_MODEL), ln2_b=ln2_b.reshape(1, D_MODEL),
    )


def _layer(x, past_k, past_v, s0, conv_prev, p, table, merge_b, *, alpha):
    n_seq, seq_len, _ = x.shape
    m = n_seq * seq_len
    x2d = x.reshape(m, D_MODEL)
    z, alo, k_new, v_new = _inproj(x2d, p["w_main"], p["w_alo"], seq_len=seq_len)
    o_gla, s_fin = _gla(z, alo, p["gup"], p["gate_b"], s0, n_seq=n_seq, seq_len=seq_len)
    keep = min(BAND_PAST, seq_len)
    k_new = k_new.reshape(n_seq, keep, ATT_HEADS, ATT_DH)
    v_new = v_new.reshape(n_seq, keep, ATT_HEADS, ATT_DH)
    if past_k is None:
        bias = _rel_bias(table, n_q=ATT_GROUP, n_k=ATT_WINDOW, off=BAND_PAST, band=True)
        o_att = _band_attention(z, bias, n_seq=n_seq, seq_len=seq_len)
    else:
        n_past = past_k.shape[1]
        n_k = ((n_past + seq_len + V7X_LANES - 1) // V7X_LANES) * V7X_LANES
        bias = _rel_bias(table, n_q=seq_len, n_k=n_k, off=n_past, band=False)
        o_att = _band_attention_step(z, past_k, past_v, bias, n_seq=n_seq, n_new=seq_len)
    h = _merge(o_gla, o_att, z, x2d, p["norm_g"], p["w_br_gla"], p["w_br_att"], p["w_out"], merge_b,
               p["ln1_g"], p["ln1_b"], alpha=alpha)
    prev = jnp.pad(conv_prev, ((0, 0), (0, 0), (0, D_FF_PAD - D_FF)))
    y, ut = _ffn(h, p["w_up_u"], p["w_up_g"], p["w_down"], p["conv_w"], p["conv_b"],
                 p["ln2_g"], p["ln2_b"], prev, alpha=alpha, n_seq=n_seq, seq_len=seq_len)
    if seq_len >= FFN_BM:
        tiles_per_seq = seq_len // FFN_BM
        conv_new = ut.reshape(n_seq, tiles_per_seq, 8, D_FF_PAD)[:, -1, 8 - (CONV_W - 1):, :D_FF]
    else:
        conv_new = ut.reshape(n_seq, seq_len, D_FF_PAD)[:, seq_len - (CONV_W - 1):, :D_FF]
    return (y.reshape(n_seq, seq_len, D_MODEL), k_new, v_new, s_fin, conv_new)


def kernel(x_prompt, x_sample, cache_att_k, cache_att_v, state_gla, state_ffn_conv, w_in, gla_gate_up,
           gla_gate_b, gla_norm_g, att_rel_bias, merge_b, w_br_gla, w_br_att, w_out, ln1_g, ln1_b,
           w_ffn_up, ffn_conv_w, ffn_conv_b, w_ffn_down, ln2_g, ln2_b):
    depth = w_in.shape[0]
    alpha = (2.0 * depth) ** 0.25
    xp, xs = x_prompt, x_sample
    n_p = xp.shape[0]
    outs_p, outs_s = [], []
    for l in range(depth):
        p = _prepare_params(w_in[l], gla_gate_up[l], gla_gate_b[l], gla_norm_g[l], w_br_gla[l],
                            w_br_att[l], w_out[l], ln1_g[l], ln1_b[l], w_ffn_up[l], ffn_conv_w[l],
                            ffn_conv_b[l], w_ffn_down[l], ln2_g[l], ln2_b[l])
        s0_p = jnp.zeros((n_p, GLA_HEADS, GLA_DK, GLA_DV), F32)
        conv0_p = jnp.zeros((n_p, CONV_W - 1, D_FF), F32)
        xp, *rest_p = _layer(xp, None, None, s0_p, conv0_p, p, att_rel_bias[l], merge_b[l], alpha=alpha)
        outs_p.append(rest_p)
        xs, *rest_s = _layer(xs, cache_att_k[l], cache_att_v[l], state_gla[l], state_ffn_conv[l], p,
                             att_rel_bias[l], merge_b[l], alpha=alpha)
        outs_s.append(rest_s)
    stack = lambda outs, idx: jnp.stack([o[idx] for o in outs])
    return (xp, xs,
            stack(outs_p, 0), stack(outs_p, 1), stack(outs_p, 2), stack(outs_p, 3),
            stack(outs_s, 0), stack(outs_s, 1), stack(outs_s, 2), stack(outs_s, 3))
```

```python
import functools
import math

import jax
import jax.numpy as jnp
from jax import lax
from jax.experimental import pallas as pl
from jax.experimental.pallas import tpu as pltpu

F32 = jnp.float32
BF16 = jnp.bfloat16

D_MODEL = 2048
CHUNK = 64
CHUNK_SHIFT = CHUNK.bit_length() - 1
assert 1 << CHUNK_SHIFT == CHUNK
GLA_HEADS = 4
GLA_QK = D_MODEL // 2
GLA_V = D_MODEL
GLA_DK = GLA_QK // GLA_HEADS
GLA_DV = GLA_V // GLA_HEADS
GLA_GATE_RANK = 16
GLA_TAU = 16.0
ATT_HEADS = 8
ATT_DH = 128
ATT_WIDTH = ATT_HEADS * ATT_DH
BAND_CHUNKS = 8
BAND_PAST = BAND_CHUNKS * CHUNK
REL_CLIP = 128
D_FF = 5504
CONV_W = 3
LN_EPS = 1e-5
RMS_EPS = 1e-6
NEG_INF = -1e30

V7X_LANES = 128
V7X_VMEM_BYTES = 64 * 1024 * 1024
VMEM_LIMIT = V7X_VMEM_BYTES - 4 * 1024 * 1024

Z_QG = 0
Z_KG = Z_QG + GLA_QK
Z_VG = Z_KG + GLA_QK
Z_RG = Z_VG + GLA_V
Z_MG = Z_RG + GLA_V
Z_MA = Z_MG + D_MODEL
Z_QA = Z_MA + D_MODEL
Z_KA = Z_QA + ATT_WIDTH
Z_VA = Z_KA + ATT_WIDTH
Z_WIDTH = Z_VA + ATT_WIDTH
ALO_PAD = V7X_LANES

INPROJ_BM = 1024
INPROJ_BN = 1024
MERGE_BM = 256
GLA_ROWS = 512
GLA_SUPER = 4 * CHUNK
FFN_BM = 1024
FFN_BM_STREAMS = 512
FFN_BF = 512
FFN_HALF = 512
FFN_ROWS = 256
D_FF_PAD = ((D_FF + FFN_BF - 1) // FFN_BF) * FFN_BF
ATT_GROUP = 4 * CHUNK
ATT_WINDOW = ATT_GROUP + BAND_PAST


def _cparams(n_axes):
    return pltpu.CompilerParams(
        dimension_semantics=("arbitrary",) * n_axes, vmem_limit_bytes=VMEM_LIMIT)


def _dot(a, b):
    return jnp.dot(a, b, preferred_element_type=F32)


def _dot_nt(a, b):
    return lax.dot_general(a, b, (((1,), (1,)), ((), ())), preferred_element_type=F32)


def _dot_tn(a, b):
    return lax.dot_general(a, b, (((0,), (0,)), ((), ())), preferred_element_type=F32)


LOG2_E = 1.4426950408889634


def _top_bits(x):
    bits = lax.bitcast_convert_type(x, jnp.uint32) & jnp.uint32(0xFFFF0000)
    return lax.bitcast_convert_type(bits, F32)


def _sigmoid(x):
    return 1.0 / (1.0 + jnp.exp(-x))


def _layer_norm(x, g, b):
    mu = jnp.mean(x, axis=-1, keepdims=True)
    xc = x - mu
    var = jnp.mean(xc * xc, axis=-1, keepdims=True)
    return xc * lax.rsqrt(var + LN_EPS) * g + b


def _inproj_kernel(x_ref, w_ref, walo_ref, z_ref, alo_ref, k32_ref, v32_ref, xb_ref, *,
                   keep_from, keep_rows, keep_every):
    i = pl.program_id(0)
    j = pl.program_id(1)

    @pl.when(j == 0)
    def _():
        xb_ref[...] = x_ref[...].astype(BF16)
        alo_ref[...] = _dot(xb_ref[...], walo_ref[...])

    res = _dot(xb_ref[...], w_ref[...])
    z_ref[...] = res.astype(BF16)

    keep_tile = lax.rem(i, keep_every) == keep_every - 1

    @pl.when(keep_tile & (j == Z_KA // INPROJ_BN))
    def _():
        k32_ref[...] = res[keep_from:keep_from + keep_rows, :]

    @pl.when(keep_tile & (j == Z_VA // INPROJ_BN))
    def _():
        v32_ref[...] = res[keep_from:keep_from + keep_rows, :]


def _inproj(x2d, w_main, w_alo, *, seq_len):
    assert INPROJ_BN == ATT_WIDTH and Z_KA % INPROJ_BN == 0 and Z_VA % INPROJ_BN == 0
    m = x2d.shape[0]
    bm = min(INPROJ_BM, m)
    keep = min(BAND_PAST, seq_len)
    if seq_len >= bm:
        assert seq_len % bm == 0 and keep <= bm
        keep_from, keep_rows, keep_every = bm - keep, keep, seq_len // bm
    else:
        assert keep == seq_len and bm % seq_len == 0
        keep_from, keep_rows, keep_every = 0, bm, 1
    n_keep = m // bm // keep_every * keep_rows
    kern = functools.partial(_inproj_kernel, keep_from=keep_from, keep_rows=keep_rows,
                             keep_every=keep_every)
    keep_spec = pl.BlockSpec((keep_rows, ATT_WIDTH), lambda i, j: (i // keep_every, 0))
    return pl.pallas_call(
        kern,
        grid=(m // bm, Z_WIDTH // INPROJ_BN),
        in_specs=[
            pl.BlockSpec((bm, D_MODEL), lambda i, j: (i, 0)),
            pl.BlockSpec((D_MODEL, INPROJ_BN), lambda i, j: (0, j)),
            pl.BlockSpec((D_MODEL, ALO_PAD), lambda i, j: (0, 0)),
        ],
        out_specs=[
            pl.BlockSpec((bm, INPROJ_BN), lambda i, j: (i, j)),
            pl.BlockSpec((bm, ALO_PAD), lambda i, j: (i, 0)),
            keep_spec,
            keep_spec,
        ],
        out_shape=[
            jax.ShapeDtypeStruct((m, Z_WIDTH), BF16),
            jax.ShapeDtypeStruct((m, ALO_PAD), F32),
            jax.ShapeDtypeStruct((n_keep, ATT_WIDTH), F32),
            jax.ShapeDtypeStruct((n_keep, ATT_WIDTH), F32),
        ],
        scratch_shapes=[pltpu.VMEM((bm, D_MODEL), BF16)],
        compiler_params=_cparams(2),
        name="inproj",
    )(x2d, w_main, w_alo)


def _gla_kernel(q_ref, k_ref, v_ref, alo_ref, gup_ref, gb_ref, s0_ref, o_ref, sfin_ref, st_ref, *,
                rows_per_step, sb, blk):
    t = pl.program_id(1)
    n_sub = sb // blk
    shift = blk.bit_length() - 1
    assert 1 << shift == blk

    @pl.when(t == 0)
    def _():
        for h in range(GLA_HEADS):
            st_ref[h] = s0_ref[0, h].T

    row = lax.broadcasted_iota(jnp.int32, (sb, sb), 0)
    col = lax.broadcasted_iota(jnp.int32, (sb, sb), 1)
    dblk = (row >> shift) - (col >> shift)
    same_blk_causal = (dblk == 0) & (row >= col)
    later_blk = {d: dblk == d for d in range(1, n_sub)}
    tril = jnp.where(same_blk_causal, 1.0, 0.0).astype(BF16)

    def rows_of(blocks):
        return jnp.concatenate([jnp.broadcast_to(r, (blk, GLA_QK)) for r in blocks], axis=0)

    def body(s, carry):
        rows = pl.ds(pl.multiple_of(s * sb, sb), sb)
        x = _dot(alo_ref[rows, :].astype(BF16), gup_ref[...]) + gb_ref[...]
        log_a = (jnp.minimum(x, 0.0) - jnp.log(1.0 + jnp.exp2(jnp.abs(x) * -LOG2_E))) * (1.0 / GLA_TAU)
        hi = _top_bits(log_a)
        rem = log_a - hi
        mid = _top_bits(rem)
        lo = rem - mid
        b = _dot(tril, hi.astype(BF16)) + _dot(tril, mid.astype(BF16)) + _dot(tril, lo.astype(BF16))
        blk_sum = [b[(j + 1) * blk - 1:(j + 1) * blk, :] for j in range(n_sub)]
        cum = [blk_sum[0]]
        for j in range(1, n_sub):
            cum.append(cum[-1] + blk_sum[j])
        total = cum[-1]
        qf = q_ref[rows, :].astype(F32) * (GLA_DK ** -0.5)
        kf = k_ref[rows, :].astype(F32)
        q_tf = qf * jnp.exp(b)
        k_decf = kf * jnp.exp(rows_of(blk_sum) - b)
        q_t = q_tf.astype(BF16)
        k_t = (kf * jnp.exp(-b)).astype(BF16)
        k_dec = k_decf.astype(BF16)
        if n_sub > 1:
            one = jnp.ones((1, GLA_QK), F32)
            q_s = (q_tf * rows_of([one] + [jnp.exp(cum[j - 1]) for j in range(1, n_sub)])).astype(BF16)
            k_s = (k_decf * rows_of([jnp.exp(total - cum[j]) for j in range(n_sub)])).astype(BF16)
            k_far = {1: k_dec}
            zero = jnp.zeros((1, GLA_QK), F32)
            for d in range(2, n_sub):
                scale = [jnp.exp(cum[j + d - 1] - cum[j]) if j + d < n_sub else zero for j in range(n_sub)]
                k_far[d] = (k_decf * rows_of(scale)).astype(BF16)
        else:
            q_s, k_s, k_far = q_t, k_dec, {}
        decay = jnp.exp(total)
        for h in range(GLA_HEADS):
            lk = slice(h * GLA_DK, (h + 1) * GLA_DK)
            lv = slice(h * GLA_DV, (h + 1) * GLA_DV)
            att = jnp.where(same_blk_causal, _dot_nt(q_t[:, lk], k_t[:, lk]), 0.0)
            for d, kd in k_far.items():
                att = jnp.where(later_blk[d], _dot_nt(q_t[:, lk], kd[:, lk]), att)
            vb = v_ref[rows, lv]
            st = st_ref[h]
            o = _dot(att.astype(BF16), vb) + _dot_nt(q_s[:, lk], st.astype(BF16))
            st_ref[h] = st * decay[:, lk] + _dot_tn(vb, k_s[:, lk])
            o_ref[rows, lv] = o.astype(BF16)
        return carry

    n_sb = rows_per_step // sb
    lax.fori_loop(0, n_sb, body, 0, unroll=2 if n_sb % 2 == 0 else 1)

    @pl.when(t == pl.num_programs(1) - 1)
    def _():
        for h in range(GLA_HEADS):
            sfin_ref[0, h] = st_ref[h].T


def _gla(z, alo, gup_pad, gate_b, s0, *, n_seq, seq_len):
    blk = min(CHUNK, seq_len)
    sb = min(GLA_SUPER, seq_len)
    rt = min(GLA_ROWS, seq_len)
    n_t = seq_len // rt
    kern = functools.partial(_gla_kernel, rows_per_step=rt, sb=sb, blk=blk)
    row_block = lambda b, t: b * n_t + t
    return pl.pallas_call(
        kern,
        grid=(n_seq, n_t),
        in_specs=[
            pl.BlockSpec((rt, GLA_QK), lambda b, t: (row_block(b, t), Z_QG // GLA_QK)),
            pl.BlockSpec((rt, GLA_QK), lambda b, t: (row_block(b, t), Z_KG // GLA_QK)),
            pl.BlockSpec((rt, GLA_V), lambda b, t: (row_block(b, t), Z_VG // GLA_V)),
            pl.BlockSpec((rt, ALO_PAD), lambda b, t: (row_block(b, t), 0)),
            pl.BlockSpec((ALO_PAD, GLA_QK), lambda b, t: (0, 0)),
            pl.BlockSpec((1, GLA_QK), lambda b, t: (0, 0)),
            pl.BlockSpec((1, GLA_HEADS, GLA_DK, GLA_DV), lambda b, t: (b, 0, 0, 0)),
        ],
        out_specs=[
            pl.BlockSpec((rt, GLA_V), lambda b, t: (row_block(b, t), 0)),
            pl.BlockSpec((1, GLA_HEADS, GLA_DK, GLA_DV), lambda b, t: (b, 0, 0, 0)),
        ],
        out_shape=[
            jax.ShapeDtypeStruct((n_seq * seq_len, GLA_V), BF16),
            jax.ShapeDtypeStruct((n_seq, GLA_HEADS, GLA_DK, GLA_DV), F32),
        ],
        scratch_shapes=[pltpu.VMEM((GLA_HEADS, GLA_DV, GLA_DK), F32)],
        compiler_params=_cparams(2),
        name="gla",
    )(z, z, z, alo, gup_pad, gate_b, s0)


def _rel_bias_kernel(tab_ref, out_ref, *, n_q, n_k, off, band):
    h = pl.program_id(0)
    k8 = lax.broadcasted_iota(jnp.int32, (8, n_k), 1)
    rel0 = jnp.clip(off - k8, -REL_CLIP, REL_CLIP) + REL_CLIP

    def body(r, acc):
        return jnp.where(rel0 == r, tab_ref[h, r], acc)

    row0 = lax.fori_loop(0, 2 * REL_CLIP + 1, body, jnp.zeros((8, n_k), F32))
    full = jnp.broadcast_to(row0[0:1, :], (n_q, n_k))
    rolled = pltpu.roll(full, 0, 1, stride=1, stride_axis=0)
    q = lax.broadcasted_iota(jnp.int32, (n_q, n_k), 0)
    k = lax.broadcasted_iota(jnp.int32, (n_q, n_k), 1)
    bias = jnp.where(k < q, tab_ref[h, 2 * REL_CLIP], rolled)
    if band:
        dc = (k >> CHUNK_SHIFT) - (q >> CHUNK_SHIFT)
        bias = jnp.where(dc < 0, NEG_INF, jnp.where(dc > BAND_CHUNKS, NEG_INF, bias))
    out_ref[0] = bias


def _rel_bias(table, *, n_q, n_k, off, band):
    assert off >= REL_CLIP and n_k % V7X_LANES == 0
    kern = functools.partial(_rel_bias_kernel, n_q=n_q, n_k=n_k, off=off, band=band)
    return pl.pallas_call(
        kern,
        grid=(ATT_HEADS,),
        in_specs=[pl.BlockSpec(memory_space=pltpu.SMEM)],
        out_specs=pl.BlockSpec((1, n_q, n_k), lambda h: (h, 0, 0)),
        out_shape=jax.ShapeDtypeStruct((ATT_HEADS, n_q, n_k), F32),
        compiler_params=_cparams(1),
        name="rel_bias",
    )(table)


def _band_kernel(q_ref, k_ref, v_ref, bias_ref, o_ref, kp_ref, vp_ref, *, seq_len):
    zeros = jnp.zeros((BAND_PAST, ATT_DH), BF16)
    kp_ref[0:BAND_PAST, :] = zeros
    vp_ref[0:BAND_PAST, :] = zeros
    kp_ref[BAND_PAST:, :] = k_ref[...]
    vp_ref[BAND_PAST:, :] = v_ref[...]
    kk = lax.broadcasted_iota(jnp.int32, (ATT_GROUP, ATT_WINDOW), 1)

    def group(q0, before_start):
        qg = q_ref[pl.ds(q0, ATT_GROUP), :]
        kw = kp_ref[pl.ds(q0, ATT_WINDOW), :]
        vw = vp_ref[pl.ds(q0, ATT_WINDOW), :]
        s = _dot_nt(qg, kw) * (ATT_DH ** -0.5) + bias_ref[0]
        if before_start:
            s = jnp.where(kk >= BAND_PAST - q0, s, NEG_INF)
        m = jnp.max(s, axis=-1, keepdims=True)
        p = jnp.exp(s - m)
        l = jnp.sum(p, axis=-1, keepdims=True)
        o = _dot(p.astype(BF16), vw) / l
        o_ref[pl.ds(q0, ATT_GROUP), :] = o.astype(BF16)

    n_groups = seq_len // ATT_GROUP
    n_head = min(-(-BAND_PAST // ATT_GROUP), n_groups)
    for g in range(n_head):
        group(g * ATT_GROUP, True)

    def body(g, carry):
        group(pl.multiple_of(g * ATT_GROUP, ATT_GROUP), False)
        return carry

    n_body = n_groups - n_head
    lax.fori_loop(n_head, n_groups, body, 0, unroll=3 if n_body % 3 == 0 else 1)


def _band_attention(z, bias, *, n_seq, seq_len):
    qb, kb, vb = Z_QA // ATT_DH, Z_KA // ATT_DH, Z_VA // ATT_DH
    kern = functools.partial(_band_kernel, seq_len=seq_len)
    return pl.pallas_call(
        kern,
        grid=(ATT_HEADS, n_seq),
        in_specs=[
            pl.BlockSpec((seq_len, ATT_DH), lambda h, b: (b, qb + h)),
            pl.BlockSpec((seq_len, ATT_DH), lambda h, b: (b, kb + h)),
            pl.BlockSpec((seq_len, ATT_DH), lambda h, b: (b, vb + h)),
            pl.BlockSpec((1, ATT_GROUP, ATT_WINDOW), lambda h, b: (h, 0, 0)),
        ],
        out_specs=pl.BlockSpec((seq_len, ATT_DH), lambda h, b: (b, h)),
        out_shape=jax.ShapeDtypeStruct((n_seq * seq_len, ATT_WIDTH), BF16),
        scratch_shapes=[pltpu.VMEM((seq_len + BAND_PAST, ATT_DH), BF16),
                        pltpu.VMEM((seq_len + BAND_PAST, ATT_DH), BF16)],
        compiler_params=_cparams(2),
        name="band_attention",
    )(z, z, z, bias)


def _band_step_kernel(q_ref, kn_ref, vn_ref, kc_ref, vc_ref, bias_ref, o_ref, *, n_new, n_past):
    scale = ATT_DH ** -0.5
    for h in range(ATT_HEADS):
        lanes = slice(h * ATT_DH, (h + 1) * ATT_DH)
        qh = q_ref[:, lanes]
        kc = kc_ref[0, pl.ds(h, n_past, stride=ATT_HEADS), :].astype(BF16)
        vc = vc_ref[0, pl.ds(h, n_past, stride=ATT_HEADS), :].astype(BF16)
        s_p = _dot_nt(qh, kc) * scale + bias_ref[h, :, 0:n_past]
        s_n = _dot_nt(qh, kn_ref[:, lanes]) * scale + bias_ref[h, :, n_past:n_past + n_new]
        m = jnp.maximum(jnp.max(s_p, axis=-1, keepdims=True), jnp.max(s_n, axis=-1, keepdims=True))
        p_p = jnp.exp(s_p - m)
        p_n = jnp.exp(s_n - m)
        l = jnp.sum(p_p, axis=-1, keepdims=True) + jnp.sum(p_n, axis=-1, keepdims=True)
        o = (_dot(p_p.astype(BF16), vc) + _dot(p_n.astype(BF16), vn_ref[:, lanes])) / l
        o_ref[:, lanes] = o.astype(BF16)


def _band_attention_step(z, cache_k, cache_v, bias, *, n_seq, n_new):
    n_past = cache_k.shape[1]
    kc = cache_k.reshape(n_seq, n_past * ATT_HEADS, ATT_DH)
    vc = cache_v.reshape(n_seq, n_past * ATT_HEADS, ATT_DH)
    qb, kb, vb = Z_QA // ATT_WIDTH, Z_KA // ATT_WIDTH, Z_VA // ATT_WIDTH
    kern = functools.partial(_band_step_kernel, n_new=n_new, n_past=n_past)
    return pl.pallas_call(
        kern,
        grid=(n_seq,),
        in_specs=[
            pl.BlockSpec((n_new, ATT_WIDTH), lambda b: (b, qb)),
            pl.BlockSpec((n_new, ATT_WIDTH), lambda b: (b, kb)),
            pl.BlockSpec((n_new, ATT_WIDTH), lambda b: (b, vb)),
            pl.BlockSpec((1, n_past * ATT_HEADS, ATT_DH), lambda b: (b, 0, 0)),
            pl.BlockSpec((1, n_past * ATT_HEADS, ATT_DH), lambda b: (b, 0, 0)),
            pl.BlockSpec(bias.shape, lambda b: (0, 0, 0)),
        ],
        out_specs=pl.BlockSpec((n_new, ATT_WIDTH), lambda b: (b, 0)),
        out_shape=jax.ShapeDtypeStruct((n_seq * n_new, ATT_WIDTH), BF16),
        compiler_params=_cparams(1),
        name="band_attention_step",
    )(z, z, z, kc, vc, bias)


def _merge_kernel(og_ref, r_ref, oa_ref, mg_ref, ma_ref, x_ref, ng_ref, wg_ref, wa_ref, wo_ref, mb_ref,
                  g_ref, b_ref, h_ref, *, alpha):
    og = og_ref[...].astype(F32)
    normed = []
    for hd in range(GLA_HEADS):
        oh = og[:, hd * GLA_DV:(hd + 1) * GLA_DV]
        ms = jnp.mean(oh * oh, axis=-1, keepdims=True)
        normed.append(oh * lax.rsqrt(ms + RMS_EPS))
    rf = r_ref[...].astype(F32)
    o_gla = (jnp.concatenate(normed, axis=1) * ng_ref[...]) * (rf * _sigmoid(rf))
    gate_g = _sigmoid(mg_ref[...].astype(F32) + mb_ref[0:1, :])
    gate_a = _sigmoid(ma_ref[...].astype(F32) + mb_ref[1:2, :])
    mixed = (gate_g * _dot(o_gla.astype(BF16), wg_ref[...])
             + gate_a * _dot(oa_ref[...], wa_ref[...]))
    mix = _dot(mixed.astype(BF16), wo_ref[...])
    h_ref[...] = _layer_norm(alpha * x_ref[...] + mix, g_ref[...], b_ref[...])


def _merge(o_gla_raw, o_att, z, x2d, norm_g, w_br_gla, w_br_att, w_out, merge_b, ln_g, ln_b, *, alpha):
    m = x2d.shape[0]
    bm = MERGE_BM
    rgb, mgb, mab = Z_RG // D_MODEL, Z_MG // D_MODEL, Z_MA // D_MODEL
    resident = dict(pipeline_mode=pl.Buffered(1))
    kern = functools.partial(_merge_kernel, alpha=alpha)
    return pl.pallas_call(
        kern,
        grid=(m // bm,),
        in_specs=[
            pl.BlockSpec((bm, GLA_V), lambda i: (i, 0)),
            pl.BlockSpec((bm, GLA_V), lambda i: (i, rgb)),
            pl.BlockSpec((bm, ATT_WIDTH), lambda i: (i, 0)),
            pl.BlockSpec((bm, D_MODEL), lambda i: (i, mgb)),
            pl.BlockSpec((bm, D_MODEL), lambda i: (i, mab)),
            pl.BlockSpec((bm, D_MODEL), lambda i: (i, 0)),
            pl.BlockSpec((1, GLA_V), lambda i: (0, 0)),
            pl.BlockSpec((GLA_V, D_MODEL), lambda i: (0, 0), **resident),
            pl.BlockSpec((ATT_WIDTH, D_MODEL), lambda i: (0, 0), **resident),
            pl.BlockSpec((D_MODEL, D_MODEL), lambda i: (0, 0), **resident),
            pl.BlockSpec((2, D_MODEL), lambda i: (0, 0)),
            pl.BlockSpec((1, D_MODEL), lambda i: (0, 0)),
            pl.BlockSpec((1, D_MODEL), lambda i: (0, 0)),
        ],
        out_specs=pl.BlockSpec((bm, D_MODEL), lambda i: (i, 0)),
        out_shape=jax.ShapeDtypeStruct((m, D_MODEL), F32),
        compiler_params=_cparams(1),
        name="merge",
    )(o_gla_raw, z, o_att, z, z, x2d, norm_g, w_br_gla, w_br_att, w_out, merge_b, ln_g, ln_b)


def _ffn_kernel(h_ref, wu_ref, wg_ref, wd_ref, cw_ref, cb_ref, g_ref, b_ref, e0_ref, e1_ref,
                y_ref, ut_ref, hb_ref, carry_ref, *, alpha, seq_len, bm, whole_u):
    i = pl.program_id(0)
    j = pl.program_id(1)

    @pl.when(j == 0)
    def _():
        hb_ref[...] = h_ref[...].astype(BF16)
        y_ref[...] = jnp.zeros(y_ref.shape, F32)

    if seq_len >= bm:
        @pl.when(i == 0)
        def _():
            carry_ref[j] = jnp.zeros(carry_ref.shape[1:], F32)

    n_col = FFN_BF // FFN_HALF
    row = lax.broadcasted_iota(jnp.int32, (FFN_ROWS, FFN_HALF), 0)
    in_stream = seq_len >= bm
    tails = [None] * n_col
    for r in range(bm // FFN_ROWS):
        rows = slice(r * FFN_ROWS, (r + 1) * FFN_ROWS)
        hb = hb_ref[rows, :]
        part = None
        for c in range(n_col):
            cols = slice(c * FFN_HALF, (c + 1) * FFN_HALF)
            u = _dot(hb, wu_ref[:, cols])
            gate = _dot(hb, wg_ref[:, cols])
            if in_stream:
                if r == 0:
                    first = lax.rem(i, seq_len // bm) == 0
                    p0 = jnp.where(first, e0_ref[0, :, cols], carry_ref[j, 6:7, cols])
                    p1 = jnp.where(first, e1_ref[0, :, cols], carry_ref[j, 7:8, cols])
                else:
                    p0, p1 = tails[c][6:7, :], tails[c][7:8, :]
                tails[c] = u[FFN_ROWS - 8:FFN_ROWS, :]
                is0 = row == 0
                is1 = row == 1
            else:
                p0 = e0_ref[rows, cols]
                p1 = e1_ref[rows, cols]
                pos = row & (seq_len - 1)
                is0 = pos == 0
                is1 = pos == 1
            u_m1 = jnp.where(is0, p1, pltpu.roll(u, 1, 0))
            u_m2 = jnp.where(is0, p0, jnp.where(is1, p1, pltpu.roll(u, 2, 0)))
            uc = (cb_ref[:, cols] + u_m2 * cw_ref[0:1, cols] + u_m1 * cw_ref[1:2, cols]
                  + u * cw_ref[2:3, cols])
            gelu = 0.5 * uc * (1.0 + jnp.tanh(math.sqrt(2.0 / math.pi) * (uc + 0.044715 * (uc * uc * uc))))
            d = _dot((gelu * gate).astype(BF16), wd_ref[cols, :])
            part = d if part is None else part + d
            if whole_u:
                ut_ref[rows, cols] = u
        y_ref[rows, :] += part
    if in_stream:
        for c in range(n_col):
            cols = slice(c * FFN_HALF, (c + 1) * FFN_HALF)
            carry_ref[j, :, cols] = tails[c]
            ut_ref[0, :, cols] = tails[c]

    @pl.when(j == pl.num_programs(1) - 1)
    def _():
        y_ref[...] = _layer_norm(alpha * h_ref[...] + y_ref[...], g_ref[...], b_ref[...])


def _ffn(h, w_up_u, w_up_g, w_down, conv_w, conv_b, ln_g, ln_b, conv_prev, *, alpha, n_seq, seq_len):
    m = h.shape[0]
    n_ff = D_FF_PAD // FFN_BF
    bm = FFN_BM if seq_len >= FFN_BM else FFN_BM_STREAMS
    if seq_len >= bm:
        tiles_per_seq = seq_len // bm
        e0 = conv_prev[:, 0:1, :]
        e1 = conv_prev[:, 1:2, :]
        e_spec = pl.BlockSpec((1, 1, FFN_BF), lambda i, j: (i // tiles_per_seq, 0, j))
        ut_shape = jax.ShapeDtypeStruct((m // bm, 8, D_FF_PAD), F32)
        ut_spec = pl.BlockSpec((1, 8, FFN_BF), lambda i, j: (i, 0, j))
        whole_u = False
    else:
        assert bm % seq_len == 0 and seq_len & (seq_len - 1) == 0 and seq_len >= CONV_W - 1
        e0 = jnp.repeat(conv_prev[:, 0, :], seq_len, axis=0)
        e1 = jnp.repeat(conv_prev[:, 1, :], seq_len, axis=0)
        e_spec = pl.BlockSpec((bm, FFN_BF), lambda i, j: (i, j))
        ut_shape = jax.ShapeDtypeStruct((m, D_FF_PAD), F32)
        ut_spec = pl.BlockSpec((bm, FFN_BF), lambda i, j: (i, j))
        whole_u = True
    kern = functools.partial(_ffn_kernel, alpha=alpha, seq_len=seq_len, bm=bm, whole_u=whole_u)
    return pl.pallas_call(
        kern,
        grid=(m // bm, n_ff),
        in_specs=[
            pl.BlockSpec((bm, D_MODEL), lambda i, j: (i, 0)),
            pl.BlockSpec((D_MODEL, FFN_BF), lambda i, j: (0, j)),
            pl.BlockSpec((D_MODEL, FFN_BF), lambda i, j: (0, j)),
            pl.BlockSpec((FFN_BF, D_MODEL), lambda i, j: (j, 0)),
            pl.BlockSpec((CONV_W, FFN_BF), lambda i, j: (0, j)),
            pl.BlockSpec((1, FFN_BF), lambda i, j: (0, j)),
            pl.BlockSpec((1, D_MODEL), lambda i, j: (0, 0)),
            pl.BlockSpec((1, D_MODEL), lambda i, j: (0, 0)),
            e_spec,
            e_spec,
        ],
        out_specs=[pl.BlockSpec((bm, D_MODEL), lambda i, j: (i, 0)), ut_spec],
        out_shape=[jax.ShapeDtypeStruct((m, D_MODEL), F32), ut_shape],
        scratch_shapes=[pltpu.VMEM((bm, D_MODEL), BF16),
                        pltpu.VMEM((n_ff, 8, FFN_BF), F32)],
        compiler_params=_cparams(2),
        name="ffn",
    )(h, w_up_u, w_up_g, w_down, conv_w, conv_b, ln_g, ln_b, e0, e1)


def _prepare_params(w_in, gla_gate_up, gla_gate_b, gla_norm_g, w_br_gla, w_br_att, w_out,
                    ln1_g, ln1_b, w_ffn_up, ffn_conv_w, ffn_conv_b, w_ffn_down, ln2_g, ln2_b):
    a0 = 2 * GLA_QK + 2 * GLA_V
    a1 = a0 + GLA_GATE_RANK
    att = w_in[:, a1:a1 + 3 * ATT_WIDTH]
    gates = w_in[:, a1 + 3 * ATT_WIDTH:]
    w_main = jnp.concatenate([w_in[:, :a0], gates, att], axis=1).astype(BF16)
    w_alo = jnp.pad(w_in[:, a0:a1], ((0, 0), (0, ALO_PAD - GLA_GATE_RANK))).astype(BF16)
    gup = jnp.pad(gla_gate_up, ((0, ALO_PAD - GLA_GATE_RANK), (0, 0))).astype(BF16)
    ffpad = D_FF_PAD - D_FF
    return dict(
        w_main=w_main, w_alo=w_alo, gup=gup,
        gate_b=gla_gate_b.reshape(1, GLA_QK), norm_g=gla_norm_g.reshape(1, GLA_V),
        w_br_gla=w_br_gla.astype(BF16), w_br_att=w_br_att.astype(BF16), w_out=w_out.astype(BF16),
        ln1_g=ln1_g.reshape(1, D_MODEL), ln1_b=ln1_b.reshape(1, D_MODEL),
        w_up_u=jnp.pad(w_ffn_up[:, :D_FF], ((0, 0), (0, ffpad))).astype(BF16),
        w_up_g=jnp.pad(w_ffn_up[:, D_FF:], ((0, 0), (0, ffpad))).astype(BF16),
        w_down=jnp.pad(w_ffn_down, ((0, ffpad), (0, 0))).astype(BF16),
        conv_w=jnp.pad(ffn_conv_w, ((0, 0), (0, ffpad))),
        conv_b=jnp.pad(ffn_conv_b.reshape(1, D_FF), ((0, 0), (0, ffpad))),
        ln2_g=ln2_g.reshape(1, D_MODEL), ln2_b=ln2_b.reshape(1, D_MODEL),
    )


def _layer(x, past_k, past_v, s0, conv_prev, p, table, merge_b, *, alpha):
    n_seq, seq_len, _ = x.shape
    m = n_seq * seq_len
    x2d = x.reshape(m, D_MODEL)
    z, alo, k_new, v_new = _inproj(x2d, p["w_main"], p["w_alo"], seq_len=seq_len)
    o_gla, s_fin = _gla(z, alo, p["gup"], p["gate_b"], s0, n_seq=n_seq, seq_len=seq_len)
    keep = min(BAND_PAST, seq_len)
    k_new = k_new.reshape(n_seq, keep, ATT_HEADS, ATT_DH)
    v_new = v_new.reshape(n_seq, keep, ATT_HEADS, ATT_DH)
    if past_k is None:
        bias = _rel_bias(table, n_q=ATT_GROUP, n_k=ATT_WINDOW, off=BAND_PAST, band=True)
        o_att = _band_attention(z, bias, n_seq=n_seq, seq_len=seq_len)
    else:
        n_past = past_k.shape[1]
        n_k = ((n_past + seq_len + V7X_LANES - 1) // V7X_LANES) * V7X_LANES
        bias = _rel_bias(table, n_q=seq_len, n_k=n_k, off=n_past, band=False)
        o_att = _band_attention_step(z, past_k, past_v, bias, n_seq=n_seq, n_new=seq_len)
    h = _merge(o_gla, o_att, z, x2d, p["norm_g"], p["w_br_gla"], p["w_br_att"], p["w_out"], merge_b,
               p["ln1_g"], p["ln1_b"], alpha=alpha)
    prev = jnp.pad(conv_prev, ((0, 0), (0, 0), (0, D_FF_PAD - D_FF)))
    y, ut = _ffn(h, p["w_up_u"], p["w_up_g"], p["w_down"], p["conv_w"], p["conv_b"],
                 p["ln2_g"], p["ln2_b"], prev, alpha=alpha, n_seq=n_seq, seq_len=seq_len)
    if seq_len >= FFN_BM:
        tiles_per_seq = seq_len // FFN_BM
        conv_new = ut.reshape(n_seq, tiles_per_seq, 8, D_FF_PAD)[:, -1, 8 - (CONV_W - 1):, :D_FF]
    else:
        conv_new = ut.reshape(n_seq, seq_len, D_FF_PAD)[:, seq_len - (CONV_W - 1):, :D_FF]
    return (y.reshape(n_seq, seq_len, D_MODEL), k_new, v_new, s_fin, conv_new)


def kernel(x_prompt, x_sample, cache_att_k, cache_att_v, state_gla, state_ffn_conv, w_in, gla_gate_up,
           gla_gate_b, gla_norm_g, att_rel_bias, merge_b, w_br_gla, w_br_att, w_out, ln1_g, ln1_b,
           w_ffn_up, ffn_conv_w, ffn_conv_b, w_ffn_down, ln2_g, ln2_b):
    depth = w_in.shape[0]
    alpha = (2.0 * depth) ** 0.25
    xp, xs = x_prompt, x_sample
    n_p = xp.shape[0]
    outs_p, outs_s = [], []
    for l in range(depth):
        p = _prepare_params(w_in[l], gla_gate_up[l], gla_gate_b[l], gla_norm_g[l], w_br_gla[l],
                            w_br_att[l], w_out[l], ln1_g[l], ln1_b[l], w_ffn_up[l], ffn_conv_w[l],
                            ffn_conv_b[l], w_ffn_down[l], ln2_g[l], ln2_b[l])
        s0_p = jnp.zeros((n_p, GLA_HEADS, GLA_DK, GLA_DV), F32)
        conv0_p = jnp.zeros((n_p, CONV_W - 1, D_FF), F32)
        xp, *rest_p = _layer(xp, None, None, s0_p, conv0_p, p, att_rel_bias[l], merge_b[l], alpha=alpha)
        outs_p.append(rest_p)
        xs, *rest_s = _layer(xs, cache_att_k[l], cache_att_v[l], state_gla[l], state_ffn_conv[l], p,
                             att_rel_bias[l], merge_b[l], alpha=alpha)
        outs_s.append(rest_s)
    stack = lambda outs, idx: jnp.stack([o[idx] for o in outs])
    return (xp, xs,
            stack(outs_p, 0), stack(outs_p, 1), stack(outs_p, 2), stack(outs_p, 3),
            stack(outs_s, 0), stack(outs_s, 1), stack(outs_s, 2), stack(outs_s, 3))
```

```python
import functools
import math

import jax
import jax.numpy as jnp
from jax import lax
from jax.experimental import pallas as pl
from jax.experimental.pallas import tpu as pltpu

F32 = jnp.float32
BF16 = jnp.bfloat16

D_MODEL = 2048
CHUNK = 64
CHUNK_SHIFT = CHUNK.bit_length() - 1
assert 1 << CHUNK_SHIFT == CHUNK
GLA_HEADS = 4
GLA_QK = D_MODEL // 2
GLA_V = D_MODEL
GLA_DK = GLA_QK // GLA_HEADS
GLA_DV = GLA_V // GLA_HEADS
GLA_GATE_RANK = 16
GLA_TAU = 16.0
ATT_HEADS = 8
ATT_DH = 128
ATT_WIDTH = ATT_HEADS * ATT_DH
BAND_CHUNKS = 8
BAND_PAST = BAND_CHUNKS * CHUNK
REL_CLIP = 128
D_FF = 5504
CONV_W = 3
LN_EPS = 1e-5
RMS_EPS = 1e-6
NEG_INF = -1e30

V7X_LANES = 128
V7X_VMEM_BYTES = 64 * 1024 * 1024
VMEM_LIMIT = V7X_VMEM_BYTES - 4 * 1024 * 1024

Z_QG = 0
Z_KG = Z_QG + GLA_QK
Z_VG = Z_KG + GLA_QK
Z_RG = Z_VG + GLA_V
Z_MG = Z_RG + GLA_V
Z_MA = Z_MG + D_MODEL
Z_QA = Z_MA + D_MODEL
Z_KA = Z_QA + ATT_WIDTH
Z_VA = Z_KA + ATT_WIDTH
Z_WIDTH = Z_VA + ATT_WIDTH
ALO_PAD = V7X_LANES

INPROJ_BM = 1024
INPROJ_BN = 1024
MERGE_BM = 256
GLA_ROWS = 512
GLA_SUPER = 4 * CHUNK
FFN_BM = 1024
FFN_BM_STREAMS = 512
FFN_BF = 512
FFN_HALF = 512
FFN_ROWS = 512
D_FF_PAD = ((D_FF + FFN_BF - 1) // FFN_BF) * FFN_BF
ATT_GROUP = 4 * CHUNK
ATT_WINDOW = ATT_GROUP + BAND_PAST


def _cparams(n_axes):
    return pltpu.CompilerParams(
        dimension_semantics=("arbitrary",) * n_axes, vmem_limit_bytes=VMEM_LIMIT)


def _dot(a, b):
    return jnp.dot(a, b, preferred_element_type=F32)


def _dot_nt(a, b):
    return lax.dot_general(a, b, (((1,), (1,)), ((), ())), preferred_element_type=F32)


def _dot_tn(a, b):
    return lax.dot_general(a, b, (((0,), (0,)), ((), ())), preferred_element_type=F32)


LOG2_E = 1.4426950408889634


def _top_bits(x):
    bits = lax.bitcast_convert_type(x, jnp.uint32) & jnp.uint32(0xFFFF0000)
    return lax.bitcast_convert_type(bits, F32)


def _sigmoid(x):
    return 1.0 / (1.0 + jnp.exp(-x))


def _layer_norm(x, g, b):
    mu = jnp.mean(x, axis=-1, keepdims=True)
    xc = x - mu
    var = jnp.mean(xc * xc, axis=-1, keepdims=True)
    return xc * lax.rsqrt(var + LN_EPS) * g + b


def _inproj_kernel(x_ref, w_ref, walo_ref, z_ref, alo_ref, k32_ref, v32_ref, xb_ref, *,
                   keep_from, keep_rows, keep_every):
    i = pl.program_id(0)
    j = pl.program_id(1)

    @pl.when(j == 0)
    def _():
        xb_ref[...] = x_ref[...].astype(BF16)
        alo_ref[...] = _dot(xb_ref[...], walo_ref[...])

    res = _dot(xb_ref[...], w_ref[...])
    z_ref[...] = res.astype(BF16)

    keep_tile = lax.rem(i, keep_every) == keep_every - 1

    @pl.when(keep_tile & (j == Z_KA // INPROJ_BN))
    def _():
        k32_ref[...] = res[keep_from:keep_from + keep_rows, :]

    @pl.when(keep_tile & (j == Z_VA // INPROJ_BN))
    def _():
        v32_ref[...] = res[keep_from:keep_from + keep_rows, :]


def _inproj(x2d, w_main, w_alo, *, seq_len):
    assert INPROJ_BN == ATT_WIDTH and Z_KA % INPROJ_BN == 0 and Z_VA % INPROJ_BN == 0
    m = x2d.shape[0]
    bm = min(INPROJ_BM, m)
    keep = min(BAND_PAST, seq_len)
    if seq_len >= bm:
        assert seq_len % bm == 0 and keep <= bm
        keep_from, keep_rows, keep_every = bm - keep, keep, seq_len // bm
    else:
        assert keep == seq_len and bm % seq_len == 0
        keep_from, keep_rows, keep_every = 0, bm, 1
    n_keep = m // bm // keep_every * keep_rows
    kern = functools.partial(_inproj_kernel, keep_from=keep_from, keep_rows=keep_rows,
                             keep_every=keep_every)
    keep_spec = pl.BlockSpec((keep_rows, ATT_WIDTH), lambda i, j: (i // keep_every, 0))
    return pl.pallas_call(
        kern,
        grid=(m // bm, Z_WIDTH // INPROJ_BN),
        in_specs=[
            pl.BlockSpec((bm, D_MODEL), lambda i, j: (i, 0)),
            pl.BlockSpec((D_MODEL, INPROJ_BN), lambda i, j: (0, j)),
            pl.BlockSpec((D_MODEL, ALO_PAD), lambda i, j: (0, 0)),
        ],
        out_specs=[
            pl.BlockSpec((bm, INPROJ_BN), lambda i, j: (i, j)),
            pl.BlockSpec((bm, ALO_PAD), lambda i, j: (i, 0)),
            keep_spec,
            keep_spec,
        ],
        out_shape=[
            jax.ShapeDtypeStruct((m, Z_WIDTH), BF16),
            jax.ShapeDtypeStruct((m, ALO_PAD), F32),
            jax.ShapeDtypeStruct((n_keep, ATT_WIDTH), F32),
            jax.ShapeDtypeStruct((n_keep, ATT_WIDTH), F32),
        ],
        scratch_shapes=[pltpu.VMEM((bm, D_MODEL), BF16)],
        compiler_params=_cparams(2),
        name="inproj",
    )(x2d, w_main, w_alo)


def _gla_kernel(q_ref, k_ref, v_ref, alo_ref, gup_ref, gb_ref, s0_ref, o_ref, sfin_ref, st_ref, *,
                rows_per_step, sb, blk):
    t = pl.program_id(1)
    n_sub = sb // blk
    shift = blk.bit_length() - 1
    assert 1 << shift == blk

    @pl.when(t == 0)
    def _():
        for h in range(GLA_HEADS):
            st_ref[h] = s0_ref[0, h].T

    row = lax.broadcasted_iota(jnp.int32, (sb, sb), 0)
    col = lax.broadcasted_iota(jnp.int32, (sb, sb), 1)
    dblk = (row >> shift) - (col >> shift)
    same_blk_causal = (dblk == 0) & (row >= col)
    later_blk = {d: dblk == d for d in range(1, n_sub)}
    tril = jnp.where(same_blk_causal, 1.0, 0.0).astype(BF16)

    def rows_of(blocks):
        return jnp.concatenate([jnp.broadcast_to(r, (blk, GLA_QK)) for r in blocks], axis=0)

    def body(s, carry):
        rows = pl.ds(pl.multiple_of(s * sb, sb), sb)
        x = _dot(alo_ref[rows, :].astype(BF16), gup_ref[...]) + gb_ref[...]
        log_a = (jnp.minimum(x, 0.0) - jnp.log(1.0 + jnp.exp2(jnp.abs(x) * -LOG2_E))) * (1.0 / GLA_TAU)
        hi = _top_bits(log_a)
        rem = log_a - hi
        mid = _top_bits(rem)
        lo = rem - mid
        b = _dot(tril, hi.astype(BF16)) + _dot(tril, mid.astype(BF16)) + _dot(tril, lo.astype(BF16))
        blk_sum = [b[(j + 1) * blk - 1:(j + 1) * blk, :] for j in range(n_sub)]
        cum = [blk_sum[0]]
        for j in range(1, n_sub):
            cum.append(cum[-1] + blk_sum[j])
        total = cum[-1]
        qf = q_ref[rows, :].astype(F32) * (GLA_DK ** -0.5)
        kf = k_ref[rows, :].astype(F32)
        q_tf = qf * jnp.exp(b)
        k_decf = kf * jnp.exp(rows_of(blk_sum) - b)
        q_t = q_tf.astype(BF16)
        k_t = (kf * jnp.exp(-b)).astype(BF16)
        k_dec = k_decf.astype(BF16)
        if n_sub > 1:
            one = jnp.ones((1, GLA_QK), F32)
            q_s = (q_tf * rows_of([one] + [jnp.exp(cum[j - 1]) for j in range(1, n_sub)])).astype(BF16)
            k_s = (k_decf * rows_of([jnp.exp(total - cum[j]) for j in range(n_sub)])).astype(BF16)
            k_far = {1: k_dec}
            zero = jnp.zeros((1, GLA_QK), F32)
            for d in range(2, n_sub):
                scale = [jnp.exp(cum[j + d - 1] - cum[j]) if j + d < n_sub else zero for j in range(n_sub)]
                k_far[d] = (k_decf * rows_of(scale)).astype(BF16)
        else:
            q_s, k_s, k_far = q_t, k_dec, {}
        decay = jnp.exp(total)
        for h in range(GLA_HEADS):
            lk = slice(h * GLA_DK, (h + 1) * GLA_DK)
            lv = slice(h * GLA_DV, (h + 1) * GLA_DV)
            att = jnp.where(same_blk_causal, _dot_nt(q_t[:, lk], k_t[:, lk]), 0.0)
            for d, kd in k_far.items():
                att = jnp.where(later_blk[d], _dot_nt(q_t[:, lk], kd[:, lk]), att)
            vb = v_ref[rows, lv]
            st = st_ref[h]
            o = _dot(att.astype(BF16), vb) + _dot_nt(q_s[:, lk], st.astype(BF16))
            st_ref[h] = st * decay[:, lk] + _dot_tn(vb, k_s[:, lk])
            o_ref[rows, lv] = o.astype(BF16)
        return carry

    n_sb = rows_per_step // sb
    lax.fori_loop(0, n_sb, body, 0, unroll=2 if n_sb % 2 == 0 else 1)

    @pl.when(t == pl.num_programs(1) - 1)
    def _():
        for h in range(GLA_HEADS):
            sfin_ref[0, h] = st_ref[h].T


def _gla(z, alo, gup_pad, gate_b, s0, *, n_seq, seq_len):
    blk = min(CHUNK, seq_len)
    sb = min(GLA_SUPER, seq_len)
    rt = min(GLA_ROWS, seq_len)
    n_t = seq_len // rt
    kern = functools.partial(_gla_kernel, rows_per_step=rt, sb=sb, blk=blk)
    row_block = lambda b, t: b * n_t + t
    return pl.pallas_call(
        kern,
        grid=(n_seq, n_t),
        in_specs=[
            pl.BlockSpec((rt, GLA_QK), lambda b, t: (row_block(b, t), Z_QG // GLA_QK)),
            pl.BlockSpec((rt, GLA_QK), lambda b, t: (row_block(b, t), Z_KG // GLA_QK)),
            pl.BlockSpec((rt, GLA_V), lambda b, t: (row_block(b, t), Z_VG // GLA_V)),
            pl.BlockSpec((rt, ALO_PAD), lambda b, t: (row_block(b, t), 0)),
            pl.BlockSpec((ALO_PAD, GLA_QK), lambda b, t: (0, 0)),
            pl.BlockSpec((1, GLA_QK), lambda b, t: (0, 0)),
            pl.BlockSpec((1, GLA_HEADS, GLA_DK, GLA_DV), lambda b, t: (b, 0, 0, 0)),
        ],
        out_specs=[
            pl.BlockSpec((rt, GLA_V), lambda b, t: (row_block(b, t), 0)),
            pl.BlockSpec((1, GLA_HEADS, GLA_DK, GLA_DV), lambda b, t: (b, 0, 0, 0)),
        ],
        out_shape=[
            jax.ShapeDtypeStruct((n_seq * seq_len, GLA_V), BF16),
            jax.ShapeDtypeStruct((n_seq, GLA_HEADS, GLA_DK, GLA_DV), F32),
        ],
        scratch_shapes=[pltpu.VMEM((GLA_HEADS, GLA_DV, GLA_DK), F32)],
        compiler_params=_cparams(2),
        name="gla",
    )(z, z, z, alo, gup_pad, gate_b, s0)


def _rel_bias_kernel(tab_ref, out_ref, *, n_q, n_k, off, band):
    h = pl.program_id(0)
    k8 = lax.broadcasted_iota(jnp.int32, (8, n_k), 1)
    rel0 = jnp.clip(off - k8, -REL_CLIP, REL_CLIP) + REL_CLIP

    def body(r, acc):
        return jnp.where(rel0 == r, tab_ref[h, r], acc)

    row0 = lax.fori_loop(0, 2 * REL_CLIP + 1, body, jnp.zeros((8, n_k), F32))
    full = jnp.broadcast_to(row0[0:1, :], (n_q, n_k))
    rolled = pltpu.roll(full, 0, 1, stride=1, stride_axis=0)
    q = lax.broadcasted_iota(jnp.int32, (n_q, n_k), 0)
    k = lax.broadcasted_iota(jnp.int32, (n_q, n_k), 1)
    bias = jnp.where(k < q, tab_ref[h, 2 * REL_CLIP], rolled)
    if band:
        dc = (k >> CHUNK_SHIFT) - (q >> CHUNK_SHIFT)
        bias = jnp.where(dc < 0, NEG_INF, jnp.where(dc > BAND_CHUNKS, NEG_INF, bias))
    out_ref[0] = bias


def _rel_bias(table, *, n_q, n_k, off, band):
    assert off >= REL_CLIP and n_k % V7X_LANES == 0
    kern = functools.partial(_rel_bias_kernel, n_q=n_q, n_k=n_k, off=off, band=band)
    return pl.pallas_call(
        kern,
        grid=(ATT_HEADS,),
        in_specs=[pl.BlockSpec(memory_space=pltpu.SMEM)],
        out_specs=pl.BlockSpec((1, n_q, n_k), lambda h: (h, 0, 0)),
        out_shape=jax.ShapeDtypeStruct((ATT_HEADS, n_q, n_k), F32),
        compiler_params=_cparams(1),
        name="rel_bias",
    )(table)


def _band_kernel(q_ref, k_ref, v_ref, bias_ref, o_ref, kp_ref, vp_ref, *, seq_len):
    zeros = jnp.zeros((BAND_PAST, ATT_DH), BF16)
    kp_ref[0:BAND_PAST, :] = zeros
    vp_ref[0:BAND_PAST, :] = zeros
    kp_ref[BAND_PAST:, :] = k_ref[...]
    vp_ref[BAND_PAST:, :] = v_ref[...]
    kk = lax.broadcasted_iota(jnp.int32, (ATT_GROUP, ATT_WINDOW), 1)

    def group(q0, before_start):
        qg = q_ref[pl.ds(q0, ATT_GROUP), :]
        kw = kp_ref[pl.ds(q0, ATT_WINDOW), :]
        vw = vp_ref[pl.ds(q0, ATT_WINDOW), :]
        s = _dot_nt(qg, kw) * (ATT_DH ** -0.5) + bias_ref[0]
        if before_start:
            s = jnp.where(kk >= BAND_PAST - q0, s, NEG_INF)
        m = jnp.max(s, axis=-1, keepdims=True)
        p = jnp.exp(s - m)
        l = jnp.sum(p, axis=-1, keepdims=True)
        o = _dot(p.astype(BF16), vw) / l
        o_ref[pl.ds(q0, ATT_GROUP), :] = o.astype(BF16)

    n_groups = seq_len // ATT_GROUP
    n_head = min(-(-BAND_PAST // ATT_GROUP), n_groups)
    for g in range(n_head):
        group(g * ATT_GROUP, True)

    def body(g, carry):
        group(pl.multiple_of(g * ATT_GROUP, ATT_GROUP), False)
        return carry

    n_body = n_groups - n_head
    lax.fori_loop(n_head, n_groups, body, 0, unroll=3 if n_body % 3 == 0 else 1)


def _band_attention(z, bias, *, n_seq, seq_len):
    qb, kb, vb = Z_QA // ATT_DH, Z_KA // ATT_DH, Z_VA // ATT_DH
    kern = functools.partial(_band_kernel, seq_len=seq_len)
    return pl.pallas_call(
        kern,
        grid=(ATT_HEADS, n_seq),
        in_specs=[
            pl.BlockSpec((seq_len, ATT_DH), lambda h, b: (b, qb + h)),
            pl.BlockSpec((seq_len, ATT_DH), lambda h, b: (b, kb + h)),
            pl.BlockSpec((seq_len, ATT_DH), lambda h, b: (b, vb + h)),
            pl.BlockSpec((1, ATT_GROUP, ATT_WINDOW), lambda h, b: (h, 0, 0)),
        ],
        out_specs=pl.BlockSpec((seq_len, ATT_DH), lambda h, b: (b, h)),
        out_shape=jax.ShapeDtypeStruct((n_seq * seq_len, ATT_WIDTH), BF16),
        scratch_shapes=[pltpu.VMEM((seq_len + BAND_PAST, ATT_DH), BF16),
                        pltpu.VMEM((seq_len + BAND_PAST, ATT_DH), BF16)],
        compiler_params=_cparams(2),
        name="band_attention",
    )(z, z, z, bias)


def _band_step_kernel(q_ref, kn_ref, vn_ref, kc_ref, vc_ref, bias_ref, o_ref, *, n_new, n_past):
    scale = ATT_DH ** -0.5
    for h in range(ATT_HEADS):
        lanes = slice(h * ATT_DH, (h + 1) * ATT_DH)
        qh = q_ref[:, lanes]
        kc = kc_ref[0, pl.ds(h, n_past, stride=ATT_HEADS), :].astype(BF16)
        vc = vc_ref[0, pl.ds(h, n_past, stride=ATT_HEADS), :].astype(BF16)
        s_p = _dot_nt(qh, kc) * scale + bias_ref[h, :, 0:n_past]
        s_n = _dot_nt(qh, kn_ref[:, lanes]) * scale + bias_ref[h, :, n_past:n_past + n_new]
        m = jnp.maximum(jnp.max(s_p, axis=-1, keepdims=True), jnp.max(s_n, axis=-1, keepdims=True))
        p_p = jnp.exp(s_p - m)
        p_n = jnp.exp(s_n - m)
        l = jnp.sum(p_p, axis=-1, keepdims=True) + jnp.sum(p_n, axis=-1, keepdims=True)
        o = (_dot(p_p.astype(BF16), vc) + _dot(p_n.astype(BF16), vn_ref[:, lanes])) / l
        o_ref[:, lanes] = o.astype(BF16)


def _band_attention_step(z, cache_k, cache_v, bias, *, n_seq, n_new):
    n_past = cache_k.shape[1]
    kc = cache_k.reshape(n_seq, n_past * ATT_HEADS, ATT_DH)
    vc = cache_v.reshape(n_seq, n_past * ATT_HEADS, ATT_DH)
    qb, kb, vb = Z_QA // ATT_WIDTH, Z_KA // ATT_WIDTH, Z_VA // ATT_WIDTH
    kern = functools.partial(_band_step_kernel, n_new=n_new, n_past=n_past)
    return pl.pallas_call(
        kern,
        grid=(n_seq,),
        in_specs=[
            pl.BlockSpec((n_new, ATT_WIDTH), lambda b: (b, qb)),
            pl.BlockSpec((n_new, ATT_WIDTH), lambda b: (b, kb)),
            pl.BlockSpec((n_new, ATT_WIDTH), lambda b: (b, vb)),
            pl.BlockSpec((1, n_past * ATT_HEADS, ATT_DH), lambda b: (b, 0, 0)),
            pl.BlockSpec((1, n_past * ATT_HEADS, ATT_DH), lambda b: (b, 0, 0)),
            pl.BlockSpec(bias.shape, lambda b: (0, 0, 0)),
        ],
        out_specs=pl.BlockSpec((n_new, ATT_WIDTH), lambda b: (b, 0)),
        out_shape=jax.ShapeDtypeStruct((n_seq * n_new, ATT_WIDTH), BF16),
        compiler_params=_cparams(1),
        name="band_attention_step",
    )(z, z, z, kc, vc, bias)


def _merge_kernel(og_ref, r_ref, oa_ref, mg_ref, ma_ref, x_ref, ng_ref, wg_ref, wa_ref, wo_ref, mb_ref,
                  g_ref, b_ref, h_ref, *, alpha):
    og = og_ref[...].astype(F32)
    normed = []
    for hd in range(GLA_HEADS):
        oh = og[:, hd * GLA_DV:(hd + 1) * GLA_DV]
        ms = jnp.mean(oh * oh, axis=-1, keepdims=True)
        normed.append(oh * lax.rsqrt(ms + RMS_EPS))
    rf = r_ref[...].astype(F32)
    o_gla = (jnp.concatenate(normed, axis=1) * ng_ref[...]) * (rf * _sigmoid(rf))
    gate_g = _sigmoid(mg_ref[...].astype(F32) + mb_ref[0:1, :])
    gate_a = _sigmoid(ma_ref[...].astype(F32) + mb_ref[1:2, :])
    mixed = (gate_g * _dot(o_gla.astype(BF16), wg_ref[...])
             + gate_a * _dot(oa_ref[...], wa_ref[...]))
    mix = _dot(mixed.astype(BF16), wo_ref[...])
    h_ref[...] = _layer_norm(alpha * x_ref[...] + mix, g_ref[...], b_ref[...])


def _merge(o_gla_raw, o_att, z, x2d, norm_g, w_br_gla, w_br_att, w_out, merge_b, ln_g, ln_b, *, alpha):
    m = x2d.shape[0]
    bm = MERGE_BM
    rgb, mgb, mab = Z_RG // D_MODEL, Z_MG // D_MODEL, Z_MA // D_MODEL
    resident = dict(pipeline_mode=pl.Buffered(1))
    kern = functools.partial(_merge_kernel, alpha=alpha)
    return pl.pallas_call(
        kern,
        grid=(m // bm,),
        in_specs=[
            pl.BlockSpec((bm, GLA_V), lambda i: (i, 0)),
            pl.BlockSpec((bm, GLA_V), lambda i: (i, rgb)),
            pl.BlockSpec((bm, ATT_WIDTH), lambda i: (i, 0)),
            pl.BlockSpec((bm, D_MODEL), lambda i: (i, mgb)),
            pl.BlockSpec((bm, D_MODEL), lambda i: (i, mab)),
            pl.BlockSpec((bm, D_MODEL), lambda i: (i, 0)),
            pl.BlockSpec((1, GLA_V), lambda i: (0, 0)),
            pl.BlockSpec((GLA_V, D_MODEL), lambda i: (0, 0), **resident),
            pl.BlockSpec((ATT_WIDTH, D_MODEL), lambda i: (0, 0), **resident),
            pl.BlockSpec((D_MODEL, D_MODEL), lambda i: (0, 0), **resident),
            pl.BlockSpec((2, D_MODEL), lambda i: (0, 0)),
            pl.BlockSpec((1, D_MODEL), lambda i: (0, 0)),
            pl.BlockSpec((1, D_MODEL), lambda i: (0, 0)),
        ],
        out_specs=pl.BlockSpec((bm, D_MODEL), lambda i: (i, 0)),
        out_shape=jax.ShapeDtypeStruct((m, D_MODEL), F32),
        compiler_params=_cparams(1),
        name="merge",
    )(o_gla_raw, z, o_att, z, z, x2d, norm_g, w_br_gla, w_br_att, w_out, merge_b, ln_g, ln_b)


def _ffn_kernel(h_ref, wu_ref, wg_ref, wd_ref, cw_ref, cb_ref, g_ref, b_ref, e0_ref, e1_ref,
                y_ref, ut_ref, hb_ref, carry_ref, *, alpha, seq_len, bm, whole_u):
    i = pl.program_id(0)
    j = pl.program_id(1)

    @pl.when(j == 0)
    def _():
        hb_ref[...] = h_ref[...].astype(BF16)
        y_ref[...] = jnp.zeros(y_ref.shape, F32)

    if seq_len >= bm:
        @pl.when(i == 0)
        def _():
            carry_ref[j] = jnp.zeros(carry_ref.shape[1:], F32)

    n_col = FFN_BF // FFN_HALF
    row = lax.broadcasted_iota(jnp.int32, (FFN_ROWS, FFN_HALF), 0)
    in_stream = seq_len >= bm
    tails = [None] * n_col
    for r in range(bm // FFN_ROWS):
        rows = slice(r * FFN_ROWS, (r + 1) * FFN_ROWS)
        hb = hb_ref[rows, :]
        part = None
        for c in range(n_col):
            cols = slice(c * FFN_HALF, (c + 1) * FFN_HALF)
            u = _dot(hb, wu_ref[:, cols])
            gate = _dot(hb, wg_ref[:, cols])
            if in_stream:
                if r == 0:
                    first = lax.rem(i, seq_len // bm) == 0
                    p0 = jnp.where(first, e0_ref[0, :, cols], carry_ref[j, 6:7, cols])
                    p1 = jnp.where(first, e1_ref[0, :, cols], carry_ref[j, 7:8, cols])
                else:
                    p0, p1 = tails[c][6:7, :], tails[c][7:8, :]
                tails[c] = u[FFN_ROWS - 8:FFN_ROWS, :]
                is0 = row == 0
                is1 = row == 1
            else:
                p0 = e0_ref[rows, cols]
                p1 = e1_ref[rows, cols]
                pos = row & (seq_len - 1)
                is0 = pos == 0
                is1 = pos == 1
            u_m1 = jnp.where(is0, p1, pltpu.roll(u, 1, 0))
            u_m2 = jnp.where(is0, p0, jnp.where(is1, p1, pltpu.roll(u, 2, 0)))
            uc = (cb_ref[:, cols] + u_m2 * cw_ref[0:1, cols] + u_m1 * cw_ref[1:2, cols]
                  + u * cw_ref[2:3, cols])
            gelu = 0.5 * uc * (1.0 + jnp.tanh(math.sqrt(2.0 / math.pi) * (uc + 0.044715 * (uc * uc * uc))))
            d = _dot((gelu * gate).astype(BF16), wd_ref[cols, :])
            part = d if part is None else part + d
            if whole_u:
                ut_ref[rows, cols] = u
        y_ref[rows, :] += part
    if in_stream:
        for c in range(n_col):
            cols = slice(c * FFN_HALF, (c + 1) * FFN_HALF)
            carry_ref[j, :, cols] = tails[c]
            ut_ref[0, :, cols] = tails[c]

    @pl.when(j == pl.num_programs(1) - 1)
    def _():
        y_ref[...] = _layer_norm(alpha * h_ref[...] + y_ref[...], g_ref[...], b_ref[...])


def _ffn(h, w_up_u, w_up_g, w_down, conv_w, conv_b, ln_g, ln_b, conv_prev, *, alpha, n_seq, seq_len):
    m = h.shape[0]
    n_ff = D_FF_PAD // FFN_BF
    bm = FFN_BM if seq_len >= FFN_BM else FFN_BM_STREAMS
    if seq_len >= bm:
        tiles_per_seq = seq_len // bm
        e0 = conv_prev[:, 0:1, :]
        e1 = conv_prev[:, 1:2, :]
        e_spec = pl.BlockSpec((1, 1, FFN_BF), lambda i, j: (i // tiles_per_seq, 0, j))
        ut_shape = jax.ShapeDtypeStruct((m // bm, 8, D_FF_PAD), F32)
        ut_spec = pl.BlockSpec((1, 8, FFN_BF), lambda i, j: (i, 0, j))
        whole_u = False
    else:
        assert bm % seq_len == 0 and seq_len & (seq_len - 1) == 0 and seq_len >= CONV_W - 1
        e0 = jnp.repeat(conv_prev[:, 0, :], seq_len, axis=0)
        e1 = jnp.repeat(conv_prev[:, 1, :], seq_len, axis=0)
        e_spec = pl.BlockSpec((bm, FFN_BF), lambda i, j: (i, j))
        ut_shape = jax.ShapeDtypeStruct((m, D_FF_PAD), F32)
        ut_spec = pl.BlockSpec((bm, FFN_BF), lambda i, j: (i, j))
        whole_u = True
    kern = functools.partial(_ffn_kernel, alpha=alpha, seq_len=seq_len, bm=bm, whole_u=whole_u)
    return pl.pallas_call(
        kern,
        grid=(m // bm, n_ff),
        in_specs=[
            pl.BlockSpec((bm, D_MODEL), lambda i, j: (i, 0)),
            pl.BlockSpec((D_MODEL, FFN_BF), lambda i, j: (0, j)),
            pl.BlockSpec((D_MODEL, FFN_BF), lambda i, j: (0, j)),
            pl.BlockSpec((FFN_BF, D_MODEL), lambda i, j: (j, 0)),
            pl.BlockSpec((CONV_W, FFN_BF), lambda i, j: (0, j)),
            pl.BlockSpec((1, FFN_BF), lambda i, j: (0, j)),
            pl.BlockSpec((1, D_MODEL), lambda i, j: (0, 0)),
            pl.BlockSpec((1, D_MODEL), lambda i, j: (0, 0)),
            e_spec,
            e_spec,
        ],
        out_specs=[pl.BlockSpec((bm, D_MODEL), lambda i, j: (i, 0)), ut_spec],
        out_shape=[jax.ShapeDtypeStruct((m, D_MODEL), F32), ut_shape],
        scratch_shapes=[pltpu.VMEM((bm, D_MODEL), BF16),
                        pltpu.VMEM((n_ff, 8, FFN_BF), F32)],
        compiler_params=_cparams(2),
        name="ffn",
    )(h, w_up_u, w_up_g, w_down, conv_w, conv_b, ln_g, ln_b, e0, e1)


def _prepare_params(w_in, gla_gate_up, gla_gate_b, gla_norm_g, w_br_gla, w_br_att, w_out,
                    ln1_g, ln1_b, w_ffn_up, ffn_conv_w, ffn_conv_b, w_ffn_down, ln2_g, ln2_b):
    a0 = 2 * GLA_QK + 2 * GLA_V
    a1 = a0 + GLA_GATE_RANK
    att = w_in[:, a1:a1 + 3 * ATT_WIDTH]
    gates = w_in[:, a1 + 3 * ATT_WIDTH:]
    w_main = jnp.concatenate([w_in[:, :a0], gates, att], axis=1).astype(BF16)
    w_alo = jnp.pad(w_in[:, a0:a1], ((0, 0), (0, ALO_PAD - GLA_GATE_RANK))).astype(BF16)
    gup = jnp.pad(gla_gate_up, ((0, ALO_PAD - GLA_GATE_RANK), (0, 0))).astype(BF16)
    ffpad = D_FF_PAD - D_FF
    return dict(
        w_main=w_main, w_alo=w_alo, gup=gup,
        gate_b=gla_gate_b.reshape(1, GLA_QK), norm_g=gla_norm_g.reshape(1, GLA_V),
        w_br_gla=w_br_gla.astype(BF16), w_br_att=w_br_att.astype(BF16), w_out=w_out.astype(BF16),
        ln1_g=ln1_g.reshape(1, D_MODEL), ln1_b=ln1_b.reshape(1, D_MODEL),
        w_up_u=jnp.pad(w_ffn_up[:, :D_FF], ((0, 0), (0, ffpad))).astype(BF16),
        w_up_g=jnp.pad(w_ffn_up[:, D_FF:], ((0, 0), (0, ffpad))).astype(BF16),
        w_down=jnp.pad(w_ffn_down, ((0, ffpad), (0, 0))).astype(BF16),
        conv_w=jnp.pad(ffn_conv_w, ((0, 0), (0, ffpad))),
        conv_b=jnp.pad(ffn_conv_b.reshape(1, D_FF), ((0, 0), (0, ffpad))),
        ln2_g=ln2_g.reshape(1, D_MODEL), ln2_b=ln2_b.reshape(1, D_MODEL),
    )


def _layer(x, past_k, past_v, s0, conv_prev, p, table, merge_b, *, alpha):
    n_seq, seq_len, _ = x.shape
    m = n_seq * seq_len
    x2d = x.reshape(m, D_MODEL)
    z, alo, k_new, v_new = _inproj(x2d, p["w_main"], p["w_alo"], seq_len=seq_len)
    o_gla, s_fin = _gla(z, alo, p["gup"], p["gate_b"], s0, n_seq=n_seq, seq_len=seq_len)
    keep = min(BAND_PAST, seq_len)
    k_new = k_new.reshape(n_seq, keep, ATT_HEADS, ATT_DH)
    v_new = v_new.reshape(n_seq, keep, ATT_HEADS, ATT_DH)
    if past_k is None:
        bias = _rel_bias(table, n_q=ATT_GROUP, n_k=ATT_WINDOW, off=BAND_PAST, band=True)
        o_att = _band_attention(z, bias, n_seq=n_seq, seq_len=seq_len)
    else:
        n_past = past_k.shape[1]
        n_k = ((n_past + seq_len + V7X_LANES - 1) // V7X_LANES) * V7X_LANES
        bias = _rel_bias(table, n_q=seq_len, n_k=n_k, off=n_past, band=False)
        o_att = _band_attention_step(z, past_k, past_v, bias, n_seq=n_seq, n_new=seq_len)
    h = _merge(o_gla, o_att, z, x2d, p["norm_g"], p["w_br_gla"], p["w_br_att"], p["w_out"], merge_b,
               p["ln1_g"], p["ln1_b"], alpha=alpha)
    prev = jnp.pad(conv_prev, ((0, 0), (0, 0), (0, D_FF_PAD - D_FF)))
    y, ut = _ffn(h, p["w_up_u"], p["w_up_g"], p["w_down"], p["conv_w"], p["conv_b"],
                 p["ln2_g"], p["ln2_b"], prev, alpha=alpha, n_seq=n_seq, seq_len=seq_len)
    if seq_len >= FFN_BM:
        tiles_per_seq = seq_len // FFN_BM
        conv_new = ut.reshape(n_seq, tiles_per_seq, 8, D_FF_PAD)[:, -1, 8 - (CONV_W - 1):, :D_FF]
    else:
        conv_new = ut.reshape(n_seq, seq_len, D_FF_PAD)[:, seq_len - (CONV_W - 1):, :D_FF]
    return (y.reshape(n_seq, seq_len, D_MODEL), k_new, v_new, s_fin, conv_new)


def kernel(x_prompt, x_sample, cache_att_k, cache_att_v, state_gla, state_ffn_conv, w_in, gla_gate_up,
           gla_gate_b, gla_norm_g, att_rel_bias, merge_b, w_br_gla, w_br_att, w_out, ln1_g, ln1_b,
           w_ffn_up, ffn_conv_w, ffn_conv_b, w_ffn_down, ln2_g, ln2_b):
    depth = w_in.shape[0]
    alpha = (2.0 * depth) ** 0.25
    xp, xs = x_prompt, x_sample
    n_p = xp.shape[0]
    outs_p, outs_s = [], []
    for l in range(depth):
        p = _prepare_params(w_in[l], gla_gate_up[l], gla_gate_b[l], gla_norm_g[l], w_br_gla[l],
                            w_br_att[l], w_out[l], ln1_g[l], ln1_b[l], w_ffn_up[l], ffn_conv_w[l],
                            ffn_conv_b[l], w_ffn_down[l], ln2_g[l], ln2_b[l])
        s0_p = jnp.zeros((n_p, GLA_HEADS, GLA_DK, GLA_DV), F32)
        conv0_p = jnp.zeros((n_p, CONV_W - 1, D_FF), F32)
        xp, *rest_p = _layer(xp, None, None, s0_p, conv0_p, p, att_rel_bias[l], merge_b[l], alpha=alpha)
        outs_p.append(rest_p)
        xs, *rest_s = _layer(xs, cache_att_k[l], cache_att_v[l], state_gla[l], state_ffn_conv[l], p,
                             att_rel_bias[l], merge_b[l], alpha=alpha)
        outs_s.append(rest_s)
    stack = lambda outs, idx: jnp.stack([o[idx] for o in outs])
    return (xp, xs,
            stack(outs_p, 0), stack(outs_p, 1), stack(outs_p, 2), stack(outs_p, 3),
            stack(outs_s, 0), stack(outs_s, 1), stack(outs_s, 2), stack(outs_s, 3))
```

```python
import functools
import math

import jax
import jax.numpy as jnp
from jax import lax
from jax.experimental import pallas as pl
from jax.experimental.pallas import tpu as pltpu

F32 = jnp.float32
BF16 = jnp.bfloat16

D_MODEL = 2048
CHUNK = 64
CHUNK_SHIFT = CHUNK.bit_length() - 1
assert 1 << CHUNK_SHIFT == CHUNK
GLA_HEADS = 4
GLA_QK = D_MODEL // 2
GLA_V = D_MODEL
GLA_DK = GLA_QK // GLA_HEADS
GLA_DV = GLA_V // GLA_HEADS
GLA_GATE_RANK = 16
GLA_TAU = 16.0
ATT_HEADS = 8
ATT_DH = 128
ATT_WIDTH = ATT_HEADS * ATT_DH
BAND_CHUNKS = 8
BAND_PAST = BAND_CHUNKS * CHUNK
REL_CLIP = 128
D_FF = 5504
CONV_W = 3
LN_EPS = 1e-5
RMS_EPS = 1e-6
NEG_INF = -1e30

V7X_LANES = 128
V7X_VMEM_BYTES = 64 * 1024 * 1024
VMEM_LIMIT = V7X_VMEM_BYTES - 4 * 1024 * 1024

Z_QG = 0
Z_KG = Z_QG + GLA_QK
Z_VG = Z_KG + GLA_QK
Z_RG = Z_VG + GLA_V
Z_MG = Z_RG + GLA_V
Z_MA = Z_MG + D_MODEL
Z_QA = Z_MA + D_MODEL
Z_KA = Z_QA + ATT_WIDTH
Z_VA = Z_KA + ATT_WIDTH
Z_WIDTH = Z_VA + ATT_WIDTH
ALO_PAD = V7X_LANES

INPROJ_BM = 1024
INPROJ_BN = 1024
MERGE_BM = 256
PREP_ROWS = 256
GLA_ROWS = 512
GLA_SUPER = 4 * CHUNK
FFN_BM = 1024
FFN_BM_STREAMS = 512
FFN_BF = 512
FFN_HALF = 512
FFN_ROWS = 512
D_FF_PAD = ((D_FF + FFN_BF - 1) // FFN_BF) * FFN_BF
ATT_GROUP = 4 * CHUNK
ATT_WINDOW = ATT_GROUP + BAND_PAST


def _cparams(n_axes):
    return pltpu.CompilerParams(
        dimension_semantics=("arbitrary",) * n_axes, vmem_limit_bytes=VMEM_LIMIT)


def _dot(a, b):
    return jnp.dot(a, b, preferred_element_type=F32)


def _dot_nt(a, b):
    return lax.dot_general(a, b, (((1,), (1,)), ((), ())), preferred_element_type=F32)


def _dot_tn(a, b):
    return lax.dot_general(a, b, (((0,), (0,)), ((), ())), preferred_element_type=F32)


LOG2_E = 1.4426950408889634


def _top_bits(x):
    bits = lax.bitcast_convert_type(x, jnp.uint32) & jnp.uint32(0xFFFF0000)
    return lax.bitcast_convert_type(bits, F32)


def _sigmoid(x):
    return 1.0 / (1.0 + jnp.exp(-x))


def _layer_norm(x, g, b):
    mu = jnp.mean(x, axis=-1, keepdims=True)
    xc = x - mu
    var = jnp.mean(xc * xc, axis=-1, keepdims=True)
    return xc * lax.rsqrt(var + LN_EPS) * g + b


def _inproj_kernel(x_ref, w_ref, walo_ref, z_ref, alo_ref, k32_ref, v32_ref, xb_ref, *,
                   keep_from, keep_rows, keep_every):
    i = pl.program_id(0)
    j = pl.program_id(1)

    @pl.when(j == 0)
    def _():
        xb_ref[...] = x_ref[...].astype(BF16)
        alo_ref[...] = _dot(xb_ref[...], walo_ref[...])

    res = _dot(xb_ref[...], w_ref[...])
    z_ref[...] = res.astype(BF16)

    keep_tile = lax.rem(i, keep_every) == keep_every - 1

    @pl.when(keep_tile & (j == Z_KA // INPROJ_BN))
    def _():
        k32_ref[...] = res[keep_from:keep_from + keep_rows, :]

    @pl.when(keep_tile & (j == Z_VA // INPROJ_BN))
    def _():
        v32_ref[...] = res[keep_from:keep_from + keep_rows, :]


def _inproj(x2d, w_main, w_alo, *, seq_len):
    assert INPROJ_BN == ATT_WIDTH and Z_KA % INPROJ_BN == 0 and Z_VA % INPROJ_BN == 0
    m = x2d.shape[0]
    bm = min(INPROJ_BM, m)
    keep = min(BAND_PAST, seq_len)
    if seq_len >= bm:
        assert seq_len % bm == 0 and keep <= bm
        keep_from, keep_rows, keep_every = bm - keep, keep, seq_len // bm
    else:
        assert keep == seq_len and bm % seq_len == 0
        keep_from, keep_rows, keep_every = 0, bm, 1
    n_keep = m // bm // keep_every * keep_rows
    kern = functools.partial(_inproj_kernel, keep_from=keep_from, keep_rows=keep_rows,
                             keep_every=keep_every)
    keep_spec = pl.BlockSpec((keep_rows, ATT_WIDTH), lambda i, j: (i // keep_every, 0))
    return pl.pallas_call(
        kern,
        grid=(m // bm, Z_WIDTH // INPROJ_BN),
        in_specs=[
            pl.BlockSpec((bm, D_MODEL), lambda i, j: (i, 0)),
            pl.BlockSpec((D_MODEL, INPROJ_BN), lambda i, j: (0, j)),
            pl.BlockSpec((D_MODEL, ALO_PAD), lambda i, j: (0, 0)),
        ],
        out_specs=[
            pl.BlockSpec((bm, INPROJ_BN), lambda i, j: (i, j)),
            pl.BlockSpec((bm, ALO_PAD), lambda i, j: (i, 0)),
            keep_spec,
            keep_spec,
        ],
        out_shape=[
            jax.ShapeDtypeStruct((m, Z_WIDTH), BF16),
            jax.ShapeDtypeStruct((m, ALO_PAD), F32),
            jax.ShapeDtypeStruct((n_keep, ATT_WIDTH), F32),
            jax.ShapeDtypeStruct((n_keep, ATT_WIDTH), F32),
        ],
        scratch_shapes=[pltpu.VMEM((bm, D_MODEL), BF16)],
        compiler_params=_cparams(2),
        name="inproj",
    )(x2d, w_main, w_alo)


def _gla_kernel(q_ref, k_ref, v_ref, alo_ref, gup_ref, gb_ref, s0_ref, o_ref, sfin_ref, st_ref, *,
                rows_per_step, sb, blk):
    t = pl.program_id(1)
    n_sub = sb // blk
    shift = blk.bit_length() - 1
    assert 1 << shift == blk

    @pl.when(t == 0)
    def _():
        for h in range(GLA_HEADS):
            st_ref[h] = s0_ref[0, h].T

    row = lax.broadcasted_iota(jnp.int32, (sb, sb), 0)
    col = lax.broadcasted_iota(jnp.int32, (sb, sb), 1)
    dblk = (row >> shift) - (col >> shift)
    same_blk_causal = (dblk == 0) & (row >= col)
    later_blk = {d: dblk == d for d in range(1, n_sub)}
    tril = jnp.where(same_blk_causal, 1.0, 0.0).astype(BF16)

    def rows_of(blocks):
        return jnp.concatenate([jnp.broadcast_to(r, (blk, GLA_QK)) for r in blocks], axis=0)

    def body(s, carry):
        rows = pl.ds(pl.multiple_of(s * sb, sb), sb)
        x = _dot(alo_ref[rows, :].astype(BF16), gup_ref[...]) + gb_ref[...]
        log_a = (jnp.minimum(x, 0.0) - jnp.log(1.0 + jnp.exp2(jnp.abs(x) * -LOG2_E))) * (1.0 / GLA_TAU)
        hi = _top_bits(log_a)
        rem = log_a - hi
        mid = _top_bits(rem)
        lo = rem - mid
        b = _dot(tril, hi.astype(BF16)) + _dot(tril, mid.astype(BF16)) + _dot(tril, lo.astype(BF16))
        blk_sum = [b[(j + 1) * blk - 1:(j + 1) * blk, :] for j in range(n_sub)]
        cum = [blk_sum[0]]
        for j in range(1, n_sub):
            cum.append(cum[-1] + blk_sum[j])
        total = cum[-1]
        qf = q_ref[rows, :].astype(F32) * (GLA_DK ** -0.5)
        kf = k_ref[rows, :].astype(F32)
        q_tf = qf * jnp.exp(b)
        k_decf = kf * jnp.exp(rows_of(blk_sum) - b)
        q_t = q_tf.astype(BF16)
        k_t = (kf * jnp.exp(-b)).astype(BF16)
        k_dec = k_decf.astype(BF16)
        if n_sub > 1:
            one = jnp.ones((1, GLA_QK), F32)
            q_s = (q_tf * rows_of([one] + [jnp.exp(cum[j - 1]) for j in range(1, n_sub)])).astype(BF16)
            k_s = (k_decf * rows_of([jnp.exp(total - cum[j]) for j in range(n_sub)])).astype(BF16)
            k_far = {1: k_dec}
            zero = jnp.zeros((1, GLA_QK), F32)
            for d in range(2, n_sub):
                scale = [jnp.exp(cum[j + d - 1] - cum[j]) if j + d < n_sub else zero for j in range(n_sub)]
                k_far[d] = (k_decf * rows_of(scale)).astype(BF16)
        else:
            q_s, k_s, k_far = q_t, k_dec, {}
        decay = jnp.exp(total)
        for h in range(GLA_HEADS):
            lk = slice(h * GLA_DK, (h + 1) * GLA_DK)
            lv = slice(h * GLA_DV, (h + 1) * GLA_DV)
            att = jnp.where(same_blk_causal, _dot_nt(q_t[:, lk], k_t[:, lk]), 0.0)
            for d, kd in k_far.items():
                att = jnp.where(later_blk[d], _dot_nt(q_t[:, lk], kd[:, lk]), att)
            vb = v_ref[rows, lv]
            st = st_ref[h]
            o = _dot(att.astype(BF16), vb) + _dot_nt(q_s[:, lk], st.astype(BF16))
            st_ref[h] = st * decay[:, lk] + _dot_tn(vb, k_s[:, lk])
            o_ref[rows, lv] = o.astype(BF16)
        return carry

    n_sb = rows_per_step // sb
    lax.fori_loop(0, n_sb, body, 0, unroll=2 if n_sb % 2 == 0 else 1)

    @pl.when(t == pl.num_programs(1) - 1)
    def _():
        for h in range(GLA_HEADS):
            sfin_ref[0, h] = st_ref[h].T


def _gla(z, alo, gup_pad, gate_b, s0, *, n_seq, seq_len):
    blk = min(CHUNK, seq_len)
    sb = min(GLA_SUPER, seq_len)
    rt = min(GLA_ROWS, seq_len)
    n_t = seq_len // rt
    kern = functools.partial(_gla_kernel, rows_per_step=rt, sb=sb, blk=blk)
    row_block = lambda b, t: b * n_t + t
    return pl.pallas_call(
        kern,
        grid=(n_seq, n_t),
        in_specs=[
            pl.BlockSpec((rt, GLA_QK), lambda b, t: (row_block(b, t), Z_QG // GLA_QK)),
            pl.BlockSpec((rt, GLA_QK), lambda b, t: (row_block(b, t), Z_KG // GLA_QK)),
            pl.BlockSpec((rt, GLA_V), lambda b, t: (row_block(b, t), Z_VG // GLA_V)),
            pl.BlockSpec((rt, ALO_PAD), lambda b, t: (row_block(b, t), 0)),
            pl.BlockSpec((ALO_PAD, GLA_QK), lambda b, t: (0, 0)),
            pl.BlockSpec((1, GLA_QK), lambda b, t: (0, 0)),
            pl.BlockSpec((1, GLA_HEADS, GLA_DK, GLA_DV), lambda b, t: (b, 0, 0, 0)),
        ],
        out_specs=[
            pl.BlockSpec((rt, GLA_V), lambda b, t: (row_block(b, t), 0)),
            pl.BlockSpec((1, GLA_HEADS, GLA_DK, GLA_DV), lambda b, t: (b, 0, 0, 0)),
        ],
        out_shape=[
            jax.ShapeDtypeStruct((n_seq * seq_len, GLA_V), BF16),
            jax.ShapeDtypeStruct((n_seq, GLA_HEADS, GLA_DK, GLA_DV), F32),
        ],
        scratch_shapes=[pltpu.VMEM((GLA_HEADS, GLA_DV, GLA_DK), F32)],
        compiler_params=_cparams(2),
        name="gla",
    )(z, z, z, alo, gup_pad, gate_b, s0)


def _rel_bias_kernel(tab_ref, out_ref, *, n_q, n_k, off, band):
    h = pl.program_id(0)
    k8 = lax.broadcasted_iota(jnp.int32, (8, n_k), 1)
    rel0 = jnp.clip(off - k8, -REL_CLIP, REL_CLIP) + REL_CLIP

    def body(r, acc):
        return jnp.where(rel0 == r, tab_ref[h, r], acc)

    row0 = lax.fori_loop(0, 2 * REL_CLIP + 1, body, jnp.zeros((8, n_k), F32))
    full = jnp.broadcast_to(row0[0:1, :], (n_q, n_k))
    rolled = pltpu.roll(full, 0, 1, stride=1, stride_axis=0)
    q = lax.broadcasted_iota(jnp.int32, (n_q, n_k), 0)
    k = lax.broadcasted_iota(jnp.int32, (n_q, n_k), 1)
    bias = jnp.where(k < q, tab_ref[h, 2 * REL_CLIP], rolled)
    if band:
        dc = (k >> CHUNK_SHIFT) - (q >> CHUNK_SHIFT)
        bias = jnp.where(dc < 0, NEG_INF, jnp.where(dc > BAND_CHUNKS, NEG_INF, bias))
    out_ref[0] = bias


def _rel_bias(table, *, n_q, n_k, off, band):
    assert off >= REL_CLIP and n_k % V7X_LANES == 0
    kern = functools.partial(_rel_bias_kernel, n_q=n_q, n_k=n_k, off=off, band=band)
    return pl.pallas_call(
        kern,
        grid=(ATT_HEADS,),
        in_specs=[pl.BlockSpec(memory_space=pltpu.SMEM)],
        out_specs=pl.BlockSpec((1, n_q, n_k), lambda h: (h, 0, 0)),
        out_shape=jax.ShapeDtypeStruct((ATT_HEADS, n_q, n_k), F32),
        compiler_params=_cparams(1),
        name="rel_bias",
    )(table)


def _band_kernel(q_ref, k_ref, v_ref, bias_ref, o_ref, kp_ref, vp_ref, *, seq_len):
    zeros = jnp.zeros((BAND_PAST, ATT_DH), BF16)
    kp_ref[0:BAND_PAST, :] = zeros
    vp_ref[0:BAND_PAST, :] = zeros
    kp_ref[BAND_PAST:, :] = k_ref[...]
    vp_ref[BAND_PAST:, :] = v_ref[...]
    kk = lax.broadcasted_iota(jnp.int32, (ATT_GROUP, ATT_WINDOW), 1)

    def group(q0, before_start):
        qg = q_ref[pl.ds(q0, ATT_GROUP), :]
        kw = kp_ref[pl.ds(q0, ATT_WINDOW), :]
        vw = vp_ref[pl.ds(q0, ATT_WINDOW), :]
        s = _dot_nt(qg, kw) * (ATT_DH ** -0.5) + bias_ref[0]
        if before_start:
            s = jnp.where(kk >= BAND_PAST - q0, s, NEG_INF)
        m = jnp.max(s, axis=-1, keepdims=True)
        p = jnp.exp(s - m)
        l = jnp.sum(p, axis=-1, keepdims=True)
        o = _dot(p.astype(BF16), vw) / l
        o_ref[pl.ds(q0, ATT_GROUP), :] = o.astype(BF16)

    n_groups = seq_len // ATT_GROUP
    n_head = min(-(-BAND_PAST // ATT_GROUP), n_groups)
    for g in range(n_head):
        group(g * ATT_GROUP, True)

    def body(g, carry):
        group(pl.multiple_of(g * ATT_GROUP, ATT_GROUP), False)
        return carry

    n_body = n_groups - n_head
    lax.fori_loop(n_head, n_groups, body, 0, unroll=3 if n_body % 3 == 0 else 1)


def _band_attention(z, bias, *, n_seq, seq_len):
    qb, kb, vb = Z_QA // ATT_DH, Z_KA // ATT_DH, Z_VA // ATT_DH
    kern = functools.partial(_band_kernel, seq_len=seq_len)
    return pl.pallas_call(
        kern,
        grid=(ATT_HEADS, n_seq),
        in_specs=[
            pl.BlockSpec((seq_len, ATT_DH), lambda h, b: (b, qb + h)),
            pl.BlockSpec((seq_len, ATT_DH), lambda h, b: (b, kb + h)),
            pl.BlockSpec((seq_len, ATT_DH), lambda h, b: (b, vb + h)),
            pl.BlockSpec((1, ATT_GROUP, ATT_WINDOW), lambda h, b: (h, 0, 0)),
        ],
        out_specs=pl.BlockSpec((seq_len, ATT_DH), lambda h, b: (b, h)),
        out_shape=jax.ShapeDtypeStruct((n_seq * seq_len, ATT_WIDTH), BF16),
        scratch_shapes=[pltpu.VMEM((seq_len + BAND_PAST, ATT_DH), BF16),
                        pltpu.VMEM((seq_len + BAND_PAST, ATT_DH), BF16)],
        compiler_params=_cparams(2),
        name="band_attention",
    )(z, z, z, bias)


def _band_step_kernel(q_ref, kn_ref, vn_ref, kc_ref, vc_ref, bias_ref, o_ref, *, n_new, n_past):
    scale = ATT_DH ** -0.5
    for h in range(ATT_HEADS):
        lanes = slice(h * ATT_DH, (h + 1) * ATT_DH)
        qh = q_ref[:, lanes]
        kc = kc_ref[0, pl.ds(h, n_past, stride=ATT_HEADS), :].astype(BF16)
        vc = vc_ref[0, pl.ds(h, n_past, stride=ATT_HEADS), :].astype(BF16)
        s_p = _dot_nt(qh, kc) * scale + bias_ref[h, :, 0:n_past]
        s_n = _dot_nt(qh, kn_ref[:, lanes]) * scale + bias_ref[h, :, n_past:n_past + n_new]
        m = jnp.maximum(jnp.max(s_p, axis=-1, keepdims=True), jnp.max(s_n, axis=-1, keepdims=True))
        p_p = jnp.exp(s_p - m)
        p_n = jnp.exp(s_n - m)
        l = jnp.sum(p_p, axis=-1, keepdims=True) + jnp.sum(p_n, axis=-1, keepdims=True)
        o = (_dot(p_p.astype(BF16), vc) + _dot(p_n.astype(BF16), vn_ref[:, lanes])) / l
        o_ref[:, lanes] = o.astype(BF16)


def _band_attention_step(z, cache_k, cache_v, bias, *, n_seq, n_new):
    n_past = cache_k.shape[1]
    kc = cache_k.reshape(n_seq, n_past * ATT_HEADS, ATT_DH)
    vc = cache_v.reshape(n_seq, n_past * ATT_HEADS, ATT_DH)
    qb, kb, vb = Z_QA // ATT_WIDTH, Z_KA // ATT_WIDTH, Z_VA // ATT_WIDTH
    kern = functools.partial(_band_step_kernel, n_new=n_new, n_past=n_past)
    return pl.pallas_call(
        kern,
        grid=(n_seq,),
        in_specs=[
            pl.BlockSpec((n_new, ATT_WIDTH), lambda b: (b, qb)),
            pl.BlockSpec((n_new, ATT_WIDTH), lambda b: (b, kb)),
            pl.BlockSpec((n_new, ATT_WIDTH), lambda b: (b, vb)),
            pl.BlockSpec((1, n_past * ATT_HEADS, ATT_DH), lambda b: (b, 0, 0)),
            pl.BlockSpec((1, n_past * ATT_HEADS, ATT_DH), lambda b: (b, 0, 0)),
            pl.BlockSpec(bias.shape, lambda b: (0, 0, 0)),
        ],
        out_specs=pl.BlockSpec((n_new, ATT_WIDTH), lambda b: (b, 0)),
        out_shape=jax.ShapeDtypeStruct((n_seq * n_new, ATT_WIDTH), BF16),
        compiler_params=_cparams(1),
        name="band_attention_step",
    )(z, z, z, kc, vc, bias)


def _merge_kernel(og_ref, r_ref, oa_ref, mg_ref, ma_ref, x_ref, ng_ref, wg_ref, wa_ref, wo_ref, mb_ref,
                  g_ref, b_ref, h_ref, *, alpha):
    og = og_ref[...].astype(F32)
    normed = []
    for hd in range(GLA_HEADS):
        oh = og[:, hd * GLA_DV:(hd + 1) * GLA_DV]
        ms = jnp.mean(oh * oh, axis=-1, keepdims=True)
        normed.append(oh * lax.rsqrt(ms + RMS_EPS))
    rf = r_ref[...].astype(F32)
    o_gla = (jnp.concatenate(normed, axis=1) * ng_ref[...]) * (rf * _sigmoid(rf))
    gate_g = _sigmoid(mg_ref[...].astype(F32) + mb_ref[0:1, :])
    gate_a = _sigmoid(ma_ref[...].astype(F32) + mb_ref[1:2, :])
    mixed = (gate_g * _dot(o_gla.astype(BF16), wg_ref[...])
             + gate_a * _dot(oa_ref[...], wa_ref[...]))
    mix = _dot(mixed.astype(BF16), wo_ref[...])
    h_ref[...] = _layer_norm(alpha * x_ref[...] + mix, g_ref[...], b_ref[...])


def _merge(o_gla_raw, o_att, z, x2d, norm_g, w_br_gla, w_br_att, w_out, merge_b, ln_g, ln_b, *, alpha):
    m = x2d.shape[0]
    bm = MERGE_BM
    rgb, mgb, mab = Z_RG // D_MODEL, Z_MG // D_MODEL, Z_MA // D_MODEL
    resident = dict(pipeline_mode=pl.Buffered(1))
    kern = functools.partial(_merge_kernel, alpha=alpha)
    return pl.pallas_call(
        kern,
        grid=(m // bm,),
        in_specs=[
            pl.BlockSpec((bm, GLA_V), lambda i: (i, 0)),
            pl.BlockSpec((bm, GLA_V), lambda i: (i, rgb)),
            pl.BlockSpec((bm, ATT_WIDTH), lambda i: (i, 0)),
            pl.BlockSpec((bm, D_MODEL), lambda i: (i, mgb)),
            pl.BlockSpec((bm, D_MODEL), lambda i: (i, mab)),
            pl.BlockSpec((bm, D_MODEL), lambda i: (i, 0)),
            pl.BlockSpec((1, GLA_V), lambda i: (0, 0)),
            pl.BlockSpec((GLA_V, D_MODEL), lambda i: (0, 0), **resident),
            pl.BlockSpec((ATT_WIDTH, D_MODEL), lambda i: (0, 0), **resident),
            pl.BlockSpec((D_MODEL, D_MODEL), lambda i: (0, 0), **resident),
            pl.BlockSpec((2, D_MODEL), lambda i: (0, 0)),
            pl.BlockSpec((1, D_MODEL), lambda i: (0, 0)),
            pl.BlockSpec((1, D_MODEL), lambda i: (0, 0)),
        ],
        out_specs=pl.BlockSpec((bm, D_MODEL), lambda i: (i, 0)),
        out_shape=jax.ShapeDtypeStruct((m, D_MODEL), F32),
        compiler_params=_cparams(1),
        name="merge",
    )(o_gla_raw, z, o_att, z, z, x2d, norm_g, w_br_gla, w_br_att, w_out, merge_b, ln_g, ln_b)


def _ffn_kernel(h_ref, wu_ref, wg_ref, wd_ref, cw_ref, cb_ref, g_ref, b_ref, e0_ref, e1_ref,
                y_ref, ut_ref, hb_ref, carry_ref, *, alpha, seq_len, bm, whole_u):
    i = pl.program_id(0)
    j = pl.program_id(1)

    @pl.when(j == 0)
    def _():
        hb_ref[...] = h_ref[...].astype(BF16)
        y_ref[...] = jnp.zeros(y_ref.shape, F32)

    if seq_len >= bm:
        @pl.when(i == 0)
        def _():
            carry_ref[j] = jnp.zeros(carry_ref.shape[1:], F32)

    n_col = FFN_BF // FFN_HALF
    row = lax.broadcasted_iota(jnp.int32, (FFN_ROWS, FFN_HALF), 0)
    in_stream = seq_len >= bm
    tails = [None] * n_col
    for r in range(bm // FFN_ROWS):
        rows = slice(r * FFN_ROWS, (r + 1) * FFN_ROWS)
        hb = hb_ref[rows, :]
        part = None
        for c in range(n_col):
            cols = slice(c * FFN_HALF, (c + 1) * FFN_HALF)
            u = _dot(hb, wu_ref[:, cols])
            gate = _dot(hb, wg_ref[:, cols])
            if in_stream:
                if r == 0:
                    first = lax.rem(i, seq_len // bm) == 0
                    p0 = jnp.where(first, e0_ref[0, :, cols], carry_ref[j, 6:7, cols])
                    p1 = jnp.where(first, e1_ref[0, :, cols], carry_ref[j, 7:8, cols])
                else:
                    p0, p1 = tails[c][6:7, :], tails[c][7:8, :]
                tails[c] = u[FFN_ROWS - 8:FFN_ROWS, :]
                is0 = row == 0
                is1 = row == 1
            else:
                p0 = e0_ref[rows, cols]
                p1 = e1_ref[rows, cols]
                pos = row & (seq_len - 1)
                is0 = pos == 0
                is1 = pos == 1
            u_m1 = jnp.where(is0, p1, pltpu.roll(u, 1, 0))
            u_m2 = jnp.where(is0, p0, jnp.where(is1, p1, pltpu.roll(u, 2, 0)))
            uc = (cb_ref[:, cols] + u_m2 * cw_ref[0:1, cols] + u_m1 * cw_ref[1:2, cols]
                  + u * cw_ref[2:3, cols])
            gelu = 0.5 * uc * (1.0 + jnp.tanh(math.sqrt(2.0 / math.pi) * (uc + 0.044715 * (uc * uc * uc))))
            d = _dot((gelu * gate).astype(BF16), wd_ref[cols, :])
            part = d if part is None else part + d
            if whole_u:
                ut_ref[rows, cols] = u
        y_ref[rows, :] += part
    if in_stream:
        for c in range(n_col):
            cols = slice(c * FFN_HALF, (c + 1) * FFN_HALF)
            carry_ref[j, :, cols] = tails[c]
            ut_ref[0, :, cols] = tails[c]

    @pl.when(j == pl.num_programs(1) - 1)
    def _():
        y_ref[...] = _layer_norm(alpha * h_ref[...] + y_ref[...], g_ref[...], b_ref[...])


def _ffn(h, w_up_u, w_up_g, w_down, conv_w, conv_b, ln_g, ln_b, conv_prev, *, alpha, n_seq, seq_len):
    m = h.shape[0]
    n_ff = D_FF_PAD // FFN_BF
    bm = FFN_BM if seq_len >= FFN_BM else FFN_BM_STREAMS
    if seq_len >= bm:
        tiles_per_seq = seq_len // bm
        e0 = conv_prev[:, 0:1, :]
        e1 = conv_prev[:, 1:2, :]
        e_spec = pl.BlockSpec((1, 1, FFN_BF), lambda i, j: (i // tiles_per_seq, 0, j))
        ut_shape = jax.ShapeDtypeStruct((m // bm, 8, D_FF_PAD), F32)
        ut_spec = pl.BlockSpec((1, 8, FFN_BF), lambda i, j: (i, 0, j))
        whole_u = False
    else:
        assert bm % seq_len == 0 and seq_len & (seq_len - 1) == 0 and seq_len >= CONV_W - 1
        e0 = jnp.repeat(conv_prev[:, 0, :], seq_len, axis=0)
        e1 = jnp.repeat(conv_prev[:, 1, :], seq_len, axis=0)
        e_spec = pl.BlockSpec((bm, FFN_BF), lambda i, j: (i, j))
        ut_shape = jax.ShapeDtypeStruct((m, D_FF_PAD), F32)
        ut_spec = pl.BlockSpec((bm, FFN_BF), lambda i, j: (i, j))
        whole_u = True
    kern = functools.partial(_ffn_kernel, alpha=alpha, seq_len=seq_len, bm=bm, whole_u=whole_u)
    return pl.pallas_call(
        kern,
        grid=(m // bm, n_ff),
        in_specs=[
            pl.BlockSpec((bm, D_MODEL), lambda i, j: (i, 0)),
            pl.BlockSpec((D_MODEL, FFN_BF), lambda i, j: (0, j)),
            pl.BlockSpec((D_MODEL, FFN_BF), lambda i, j: (0, j)),
            pl.BlockSpec((FFN_BF, D_MODEL), lambda i, j: (j, 0)),
            pl.BlockSpec((CONV_W, FFN_BF), lambda i, j: (0, j)),
            pl.BlockSpec((1, FFN_BF), lambda i, j: (0, j)),
            pl.BlockSpec((1, D_MODEL), lambda i, j: (0, 0)),
            pl.BlockSpec((1, D_MODEL), lambda i, j: (0, 0)),
            e_spec,
            e_spec,
        ],
        out_specs=[pl.BlockSpec((bm, D_MODEL), lambda i, j: (i, 0)), ut_spec],
        out_shape=[jax.ShapeDtypeStruct((m, D_MODEL), F32), ut_shape],
        scratch_shapes=[pltpu.VMEM((bm, D_MODEL), BF16),
                        pltpu.VMEM((n_ff, 8, FFN_BF), F32)],
        compiler_params=_cparams(2),
        name="ffn",
    )(h, w_up_u, w_up_g, w_down, conv_w, conv_b, ln_g, ln_b, e0, e1)


def _relayout_w_in_kernel(w_ref, main_ref, alo_ref):
    a0 = Z_MG
    a1 = a0 + GLA_GATE_RANK
    n_att = 3 * ATT_WIDTH
    main_ref[:, 0:a0] = w_ref[:, 0:a0].astype(BF16)
    main_ref[:, Z_MG:Z_QA] = w_ref[:, a1 + n_att:a1 + n_att + 2 * D_MODEL].astype(BF16)
    main_ref[:, Z_QA:Z_WIDTH] = w_ref[:, a1:a1 + n_att].astype(BF16)
    lane = lax.broadcasted_iota(jnp.int32, alo_ref.shape, 1)
    alo_ref[...] = jnp.where(lane < GLA_GATE_RANK, w_ref[:, a0:a0 + ALO_PAD], 0.0).astype(BF16)


def _relayout_w_in(w_in):
    k, width = w_in.shape
    return pl.pallas_call(
        _relayout_w_in_kernel,
        grid=(k // PREP_ROWS,),
        in_specs=[pl.BlockSpec((PREP_ROWS, width), lambda i: (i, 0))],
        out_specs=[pl.BlockSpec((PREP_ROWS, Z_WIDTH), lambda i: (i, 0)),
                   pl.BlockSpec((PREP_ROWS, ALO_PAD), lambda i: (i, 0))],
        out_shape=[jax.ShapeDtypeStruct((k, Z_WIDTH), BF16), jax.ShapeDtypeStruct((k, ALO_PAD), BF16)],
        compiler_params=_cparams(1),
        name="relayout_w_in",
    )(w_in)


def _split_w_up_kernel(w_ref, u_ref, g_ref):
    pad = jnp.zeros((w_ref.shape[0], D_FF_PAD - D_FF), BF16)
    u_ref[:, 0:D_FF] = w_ref[:, 0:D_FF].astype(BF16)
    u_ref[:, D_FF:D_FF_PAD] = pad
    g_ref[:, 0:D_FF] = w_ref[:, D_FF:2 * D_FF].astype(BF16)
    g_ref[:, D_FF:D_FF_PAD] = pad


def _split_w_up(w_ffn_up):
    k = w_ffn_up.shape[0]
    half = pl.BlockSpec((PREP_ROWS, D_FF_PAD), lambda i: (i, 0))
    return pl.pallas_call(
        _split_w_up_kernel,
        grid=(k // PREP_ROWS,),
        in_specs=[pl.BlockSpec((PREP_ROWS, 2 * D_FF), lambda i: (i, 0))],
        out_specs=[half, half],
        out_shape=[jax.ShapeDtypeStruct((k, D_FF_PAD), BF16)] * 2,
        compiler_params=_cparams(1),
        name="split_w_up",
    )(w_ffn_up)


def _prepare_params(w_in, gla_gate_up, gla_gate_b, gla_norm_g, w_br_gla, w_br_att, w_out,
                    ln1_g, ln1_b, w_ffn_up, ffn_conv_w, ffn_conv_b, w_ffn_down, ln2_g, ln2_b):
    w_main, w_alo = _relayout_w_in(w_in)
    w_up_u, w_up_g = _split_w_up(w_ffn_up)
    gup = jnp.pad(gla_gate_up, ((0, ALO_PAD - GLA_GATE_RANK), (0, 0))).astype(BF16)
    ffpad = D_FF_PAD - D_FF
    return dict(
        w_main=w_main, w_alo=w_alo, gup=gup,
        gate_b=gla_gate_b.reshape(1, GLA_QK), norm_g=gla_norm_g.reshape(1, GLA_V),
        w_br_gla=w_br_gla.astype(BF16), w_br_att=w_br_att.astype(BF16), w_out=w_out.astype(BF16),
        ln1_g=ln1_g.reshape(1, D_MODEL), ln1_b=ln1_b.reshape(1, D_MODEL),
        w_up_u=w_up_u, w_up_g=w_up_g,
        w_down=jnp.pad(w_ffn_down, ((0, ffpad), (0, 0))).astype(BF16),
        conv_w=jnp.pad(ffn_conv_w, ((0, 0), (0, ffpad))),
        conv_b=jnp.pad(ffn_conv_b.reshape(1, D_FF), ((0, 0), (0, ffpad))),
        ln2_g=ln2_g.reshape(1, D_MODEL), ln2_b=ln2_b.reshape(1, D_MODEL),
    )


def _layer(x, past_k, past_v, s0, conv_prev, p, table, merge_b, *, alpha):
    n_seq, seq_len, _ = x.shape
    m = n_seq * seq_len
    x2d = x.reshape(m, D_MODEL)
    z, alo, k_new, v_new = _inproj(x2d, p["w_main"], p["w_alo"], seq_len=seq_len)
    o_gla, s_fin = _gla(z, alo, p["gup"], p["gate_b"], s0, n_seq=n_seq, seq_len=seq_len)
    keep = min(BAND_PAST, seq_len)
    k_new = k_new.reshape(n_seq, keep, ATT_HEADS, ATT_DH)
    v_new = v_new.reshape(n_seq, keep, ATT_HEADS, ATT_DH)
    if past_k is None:
        bias = _rel_bias(table, n_q=ATT_GROUP, n_k=ATT_WINDOW, off=BAND_PAST, band=True)
        o_att = _band_attention(z, bias, n_seq=n_seq, seq_len=seq_len)
    else:
        n_past = past_k.shape[1]
        n_k = ((n_past + seq_len + V7X_LANES - 1) // V7X_LANES) * V7X_LANES
        bias = _rel_bias(table, n_q=seq_len, n_k=n_k, off=n_past, band=False)
        o_att = _band_attention_step(z, past_k, past_v, bias, n_seq=n_seq, n_new=seq_len)
    h = _merge(o_gla, o_att, z, x2d, p["norm_g"], p["w_br_gla"], p["w_br_att"], p["w_out"], merge_b,
               p["ln1_g"], p["ln1_b"], alpha=alpha)
    prev = jnp.pad(conv_prev, ((0, 0), (0, 0), (0, D_FF_PAD - D_FF)))
    y, ut = _ffn(h, p["w_up_u"], p["w_up_g"], p["w_down"], p["conv_w"], p["conv_b"],
                 p["ln2_g"], p["ln2_b"], prev, alpha=alpha, n_seq=n_seq, seq_len=seq_len)
    if seq_len >= FFN_BM:
        tiles_per_seq = seq_len // FFN_BM
        conv_new = ut.reshape(n_seq, tiles_per_seq, 8, D_FF_PAD)[:, -1, 8 - (CONV_W - 1):, :D_FF]
    else:
        conv_new = ut.reshape(n_seq, seq_len, D_FF_PAD)[:, seq_len - (CONV_W - 1):, :D_FF]
    return (y.reshape(n_seq, seq_len, D_MODEL), k_new, v_new, s_fin, conv_new)


def kernel(x_prompt, x_sample, cache_att_k, cache_att_v, state_gla, state_ffn_conv, w_in, gla_gate_up,
           gla_gate_b, gla_norm_g, att_rel_bias, merge_b, w_br_gla, w_br_att, w_out, ln1_g, ln1_b,
           w_ffn_up, ffn_conv_w, ffn_conv_b, w_ffn_down, ln2_g, ln2_b):
    depth = w_in.shape[0]
    alpha = (2.0 * depth) ** 0.25
    xp, xs = x_prompt, x_sample
    n_p = xp.shape[0]
    outs_p, outs_s = [], []
    for l in range(depth):
        p = _prepare_params(w_in[l], gla_gate_up[l], gla_gate_b[l], gla_norm_g[l], w_br_gla[l],
                            w_br_att[l], w_out[l], ln1_g[l], ln1_b[l], w_ffn_up[l], ffn_conv_w[l],
                            ffn_conv_b[l], w_ffn_down[l], ln2_g[l], ln2_b[l])
        s0_p = jnp.zeros((n_p, GLA_HEADS, GLA_DK, GLA_DV), F32)
        conv0_p = jnp.zeros((n_p, CONV_W - 1, D_FF), F32)
        xp, *rest_p = _layer(xp, None, None, s0_p, conv0_p, p, att_rel_bias[l], merge_b[l], alpha=alpha)
        outs_p.append(rest_p)
        xs, *rest_s = _layer(xs, cache_att_k[l], cache_att_v[l], state_gla[l], state_ffn_conv[l], p,
                             att_rel_bias[l], merge_b[l], alpha=alpha)
        outs_s.append(rest_s)
    stack = lambda outs, idx: jnp.stack([o[idx] for o in outs])
    return (xp, xs,
            stack(outs_p, 0), stack(outs_p, 1), stack(outs_p, 2), stack(outs_p, 3),
            stack(outs_s, 0), stack(outs_s, 1), stack(outs_s, 2), stack(outs_s, 3))
```

```python
import functools
import math

import jax
import jax.numpy as jnp
from jax import lax
from jax.experimental import pallas as pl
from jax.experimental.pallas import tpu as pltpu

F32 = jnp.float32
BF16 = jnp.bfloat16

D_MODEL = 2048
CHUNK = 64
CHUNK_SHIFT = CHUNK.bit_length() - 1
assert 1 << CHUNK_SHIFT == CHUNK
GLA_HEADS = 4
GLA_QK = D_MODEL // 2
GLA_V = D_MODEL
GLA_DK = GLA_QK // GLA_HEADS
GLA_DV = GLA_V // GLA_HEADS
GLA_GATE_RANK = 16
GLA_TAU = 16.0
ATT_HEADS = 8
ATT_DH = 128
ATT_WIDTH = ATT_HEADS * ATT_DH
BAND_CHUNKS = 8
BAND_PAST = BAND_CHUNKS * CHUNK
REL_CLIP = 128
D_FF = 5504
CONV_W = 3
LN_EPS = 1e-5
RMS_EPS = 1e-6
NEG_INF = -1e30

V7X_LANES = 128
V7X_VMEM_BYTES = 64 * 1024 * 1024
VMEM_LIMIT = V7X_VMEM_BYTES - 4 * 1024 * 1024

Z_QG = 0
Z_KG = Z_QG + GLA_QK
Z_VG = Z_KG + GLA_QK
Z_RG = Z_VG + GLA_V
Z_MG = Z_RG + GLA_V
Z_MA = Z_MG + D_MODEL
Z_QA = Z_MA + D_MODEL
Z_KA = Z_QA + ATT_WIDTH
Z_VA = Z_KA + ATT_WIDTH
Z_WIDTH = Z_VA + ATT_WIDTH
ALO_PAD = V7X_LANES

INPROJ_BM = 1024
INPROJ_BN = 1024
MERGE_BM = 256
PREP_ROWS = 256
GLA_ROWS = 512
GLA_SUPER = 4 * CHUNK
FFN_BM = 1024
FFN_BM_STREAMS = 512
FFN_BF = 512
FFN_HALF = 512
FFN_ROWS = 512
D_FF_PAD = ((D_FF + FFN_BF - 1) // FFN_BF) * FFN_BF
ATT_GROUP = 4 * CHUNK
ATT_WINDOW = ATT_GROUP + BAND_PAST


def _cparams(n_axes):
    return pltpu.CompilerParams(
        dimension_semantics=("arbitrary",) * n_axes, vmem_limit_bytes=VMEM_LIMIT)


def _dot(a, b):
    return jnp.dot(a, b, preferred_element_type=F32)


def _dot_nt(a, b):
    return lax.dot_general(a, b, (((1,), (1,)), ((), ())), preferred_element_type=F32)


def _dot_tn(a, b):
    return lax.dot_general(a, b, (((0,), (0,)), ((), ())), preferred_element_type=F32)


LOG2_E = 1.4426950408889634


def _top_bits(x):
    bits = lax.bitcast_convert_type(x, jnp.uint32) & jnp.uint32(0xFFFF0000)
    return lax.bitcast_convert_type(bits, F32)


def _sigmoid(x):
    return 1.0 / (1.0 + jnp.exp(-x))


def _layer_norm(x, g, b):
    mu = jnp.mean(x, axis=-1, keepdims=True)
    xc = x - mu
    var = jnp.mean(xc * xc, axis=-1, keepdims=True)
    return xc * lax.rsqrt(var + LN_EPS) * g + b


def _inproj_kernel(x_ref, w_ref, walo_ref, z_ref, alo_ref, k32_ref, v32_ref, xb_ref, *,
                   keep_from, keep_rows, keep_every):
    i = pl.program_id(0)
    j = pl.program_id(1)

    @pl.when(j == 0)
    def _():
        xb_ref[...] = x_ref[...].astype(BF16)
        alo_ref[...] = _dot(xb_ref[...], walo_ref[...])

    res = _dot(xb_ref[...], w_ref[...])
    z_ref[...] = res.astype(BF16)

    keep_tile = lax.rem(i, keep_every) == keep_every - 1

    @pl.when(keep_tile & (j == Z_KA // INPROJ_BN))
    def _():
        k32_ref[...] = res[keep_from:keep_from + keep_rows, :]

    @pl.when(keep_tile & (j == Z_VA // INPROJ_BN))
    def _():
        v32_ref[...] = res[keep_from:keep_from + keep_rows, :]


def _inproj(x2d, w_main, w_alo, *, seq_len):
    assert INPROJ_BN == ATT_WIDTH and Z_KA % INPROJ_BN == 0 and Z_VA % INPROJ_BN == 0
    m = x2d.shape[0]
    bm = min(INPROJ_BM, m)
    keep = min(BAND_PAST, seq_len)
    if seq_len >= bm:
        assert seq_len % bm == 0 and keep <= bm
        keep_from, keep_rows, keep_every = bm - keep, keep, seq_len // bm
    else:
        assert keep == seq_len and bm % seq_len == 0
        keep_from, keep_rows, keep_every = 0, bm, 1
    n_keep = m // bm // keep_every * keep_rows
    kern = functools.partial(_inproj_kernel, keep_from=keep_from, keep_rows=keep_rows,
                             keep_every=keep_every)
    keep_spec = pl.BlockSpec((keep_rows, ATT_WIDTH), lambda i, j: (i // keep_every, 0))
    return pl.pallas_call(
        kern,
        grid=(m // bm, Z_WIDTH // INPROJ_BN),
        in_specs=[
            pl.BlockSpec((bm, D_MODEL), lambda i, j: (i, 0)),
            pl.BlockSpec((D_MODEL, INPROJ_BN), lambda i, j: (0, j)),
            pl.BlockSpec((D_MODEL, ALO_PAD), lambda i, j: (0, 0)),
        ],
        out_specs=[
            pl.BlockSpec((bm, INPROJ_BN), lambda i, j: (i, j)),
            pl.BlockSpec((bm, ALO_PAD), lambda i, j: (i, 0)),
            keep_spec,
            keep_spec,
        ],
        out_shape=[
            jax.ShapeDtypeStruct((m, Z_WIDTH), BF16),
            jax.ShapeDtypeStruct((m, ALO_PAD), F32),
            jax.ShapeDtypeStruct((n_keep, ATT_WIDTH), F32),
            jax.ShapeDtypeStruct((n_keep, ATT_WIDTH), F32),
        ],
        scratch_shapes=[pltpu.VMEM((bm, D_MODEL), BF16)],
        compiler_params=_cparams(2),
        name="inproj",
    )(x2d, w_main, w_alo)


def _gla_kernel(q_ref, k_ref, v_ref, alo_ref, gup_ref, gb_ref, s0_ref, o_ref, sfin_ref, st_ref, *,
                rows_per_step, sb, blk):
    t = pl.program_id(1)
    n_sub = sb // blk
    shift = blk.bit_length() - 1
    assert 1 << shift == blk

    @pl.when(t == 0)
    def _():
        for h in range(GLA_HEADS):
            st_ref[h] = s0_ref[0, h].T

    row = lax.broadcasted_iota(jnp.int32, (sb, sb), 0)
    col = lax.broadcasted_iota(jnp.int32, (sb, sb), 1)
    dblk = (row >> shift) - (col >> shift)
    same_blk_causal = (dblk == 0) & (row >= col)
    later_blk = {d: dblk == d for d in range(1, n_sub)}
    tril = jnp.where(same_blk_causal, 1.0, 0.0).astype(BF16)

    def rows_of(blocks):
        return jnp.concatenate([jnp.broadcast_to(r, (blk, GLA_QK)) for r in blocks], axis=0)

    def body(s, carry):
        rows = pl.ds(pl.multiple_of(s * sb, sb), sb)
        x = _dot(alo_ref[rows, :].astype(BF16), gup_ref[...]) + gb_ref[...]
        log_a = (jnp.minimum(x, 0.0) - jnp.log(1.0 + jnp.exp2(jnp.abs(x) * -LOG2_E))) * (1.0 / GLA_TAU)
        hi = _top_bits(log_a)
        rem = log_a - hi
        mid = _top_bits(rem)
        lo = rem - mid
        b = _dot(tril, hi.astype(BF16)) + _dot(tril, mid.astype(BF16)) + _dot(tril, lo.astype(BF16))
        blk_sum = [b[(j + 1) * blk - 1:(j + 1) * blk, :] for j in range(n_sub)]
        cum = [blk_sum[0]]
        for j in range(1, n_sub):
            cum.append(cum[-1] + blk_sum[j])
        total = cum[-1]
        qf = q_ref[rows, :].astype(F32) * (GLA_DK ** -0.5)
        kf = k_ref[rows, :].astype(F32)
        q_tf = qf * jnp.exp(b)
        k_decf = kf * jnp.exp(rows_of(blk_sum) - b)
        q_t = q_tf.astype(BF16)
        k_t = (kf * jnp.exp(-b)).astype(BF16)
        k_dec = k_decf.astype(BF16)
        if n_sub > 1:
            one = jnp.ones((1, GLA_QK), F32)
            q_s = (q_tf * rows_of([one] + [jnp.exp(cum[j - 1]) for j in range(1, n_sub)])).astype(BF16)
            k_s = (k_decf * rows_of([jnp.exp(total - cum[j]) for j in range(n_sub)])).astype(BF16)
            k_far = {1: k_dec}
            zero = jnp.zeros((1, GLA_QK), F32)
            for d in range(2, n_sub):
                scale = [jnp.exp(cum[j + d - 1] - cum[j]) if j + d < n_sub else zero for j in range(n_sub)]
                k_far[d] = (k_decf * rows_of(scale)).astype(BF16)
        else:
            q_s, k_s, k_far = q_t, k_dec, {}
        decay = jnp.exp(total)
        for h in range(GLA_HEADS):
            lk = slice(h * GLA_DK, (h + 1) * GLA_DK)
            lv = slice(h * GLA_DV, (h + 1) * GLA_DV)
            att = jnp.where(same_blk_causal, _dot_nt(q_t[:, lk], k_t[:, lk]), 0.0)
            for d, kd in k_far.items():
                att = jnp.where(later_blk[d], _dot_nt(q_t[:, lk], kd[:, lk]), att)
            vb = v_ref[rows, lv]
            st = st_ref[h]
            o = _dot(att.astype(BF16), vb) + _dot_nt(q_s[:, lk], st.astype(BF16))
            st_ref[h] = st * decay[:, lk] + _dot_tn(vb, k_s[:, lk])
            o_ref[rows, lv] = o.astype(BF16)
        return carry

    n_sb = rows_per_step // sb
    lax.fori_loop(0, n_sb, body, 0, unroll=2 if n_sb % 2 == 0 else 1)

    @pl.when(t == pl.num_programs(1) - 1)
    def _():
        for h in range(GLA_HEADS):
            sfin_ref[0, h] = st_ref[h].T


def _gla(z, alo, gup_pad, gate_b, s0, *, n_seq, seq_len):
    blk = min(CHUNK, seq_len)
    sb = min(GLA_SUPER, seq_len)
    rt = min(GLA_ROWS, seq_len)
    n_t = seq_len // rt
    kern = functools.partial(_gla_kernel, rows_per_step=rt, sb=sb, blk=blk)
    row_block = lambda b, t: b * n_t + t
    return pl.pallas_call(
        kern,
        grid=(n_seq, n_t),
        in_specs=[
            pl.BlockSpec((rt, GLA_QK), lambda b, t: (row_block(b, t), Z_QG // GLA_QK)),
            pl.BlockSpec((rt, GLA_QK), lambda b, t: (row_block(b, t), Z_KG // GLA_QK)),
            pl.BlockSpec((rt, GLA_V), lambda b, t: (row_block(b, t), Z_VG // GLA_V)),
            pl.BlockSpec((rt, ALO_PAD), lambda b, t: (row_block(b, t), 0)),
            pl.BlockSpec((ALO_PAD, GLA_QK), lambda b, t: (0, 0)),
            pl.BlockSpec((1, GLA_QK), lambda b, t: (0, 0)),
            pl.BlockSpec((1, GLA_HEADS, GLA_DK, GLA_DV), lambda b, t: (b, 0, 0, 0)),
        ],
        out_specs=[
            pl.BlockSpec((rt, GLA_V), lambda b, t: (row_block(b, t), 0)),
            pl.BlockSpec((1, GLA_HEADS, GLA_DK, GLA_DV), lambda b, t: (b, 0, 0, 0)),
        ],
        out_shape=[
            jax.ShapeDtypeStruct((n_seq * seq_len, GLA_V), BF16),
            jax.ShapeDtypeStruct((n_seq, GLA_HEADS, GLA_DK, GLA_DV), F32),
        ],
        scratch_shapes=[pltpu.VMEM((GLA_HEADS, GLA_DV, GLA_DK), F32)],
        compiler_params=_cparams(2),
        name="gla",
    )(z, z, z, alo, gup_pad, gate_b, s0)


def _rel_bias_kernel(tab_ref, out_ref, *, n_q, n_k, off, band):
    h = pl.program_id(0)
    k8 = lax.broadcasted_iota(jnp.int32, (8, n_k), 1)
    rel0 = jnp.clip(off - k8, -REL_CLIP, REL_CLIP) + REL_CLIP

    def body(r, acc):
        return jnp.where(rel0 == r, tab_ref[h, r], acc)

    row0 = lax.fori_loop(0, 2 * REL_CLIP + 1, body, jnp.zeros((8, n_k), F32))
    full = jnp.broadcast_to(row0[0:1, :], (n_q, n_k))
    rolled = pltpu.roll(full, 0, 1, stride=1, stride_axis=0)
    q = lax.broadcasted_iota(jnp.int32, (n_q, n_k), 0)
    k = lax.broadcasted_iota(jnp.int32, (n_q, n_k), 1)
    bias = jnp.where(k < q, tab_ref[h, 2 * REL_CLIP], rolled)
    if band:
        dc = (k >> CHUNK_SHIFT) - (q >> CHUNK_SHIFT)
        bias = jnp.where(dc < 0, NEG_INF, jnp.where(dc > BAND_CHUNKS, NEG_INF, bias))
    out_ref[0] = bias


def _rel_bias(table, *, n_q, n_k, off, band):
    assert off >= REL_CLIP and n_k % V7X_LANES == 0
    kern = functools.partial(_rel_bias_kernel, n_q=n_q, n_k=n_k, off=off, band=band)
    return pl.pallas_call(
        kern,
        grid=(ATT_HEADS,),
        in_specs=[pl.BlockSpec(memory_space=pltpu.SMEM)],
        out_specs=pl.BlockSpec((1, n_q, n_k), lambda h: (h, 0, 0)),
        out_shape=jax.ShapeDtypeStruct((ATT_HEADS, n_q, n_k), F32),
        compiler_params=_cparams(1),
        name="rel_bias",
    )(table)


def _band_kernel(q_ref, k_ref, v_ref, bias_ref, o_ref, kp_ref, vp_ref, *, seq_len):
    zeros = jnp.zeros((BAND_PAST, ATT_DH), BF16)
    kp_ref[0:BAND_PAST, :] = zeros
    vp_ref[0:BAND_PAST, :] = zeros
    kp_ref[BAND_PAST:, :] = k_ref[...]
    vp_ref[BAND_PAST:, :] = v_ref[...]
    kk = lax.broadcasted_iota(jnp.int32, (ATT_GROUP, ATT_WINDOW), 1)

    def group(q0, before_start):
        qg = q_ref[pl.ds(q0, ATT_GROUP), :]
        kw = kp_ref[pl.ds(q0, ATT_WINDOW), :]
        vw = vp_ref[pl.ds(q0, ATT_WINDOW), :]
        s = _dot_nt(qg, kw) * (ATT_DH ** -0.5) + bias_ref[0]
        if before_start:
            s = jnp.where(kk >= BAND_PAST - q0, s, NEG_INF)
        m = jnp.max(s, axis=-1, keepdims=True)
        p = jnp.exp(s - m)
        l = jnp.sum(p, axis=-1, keepdims=True)
        o = _dot(p.astype(BF16), vw) / l
        o_ref[pl.ds(q0, ATT_GROUP), :] = o.astype(BF16)

    n_groups = seq_len // ATT_GROUP
    n_head = min(-(-BAND_PAST // ATT_GROUP), n_groups)
    for g in range(n_head):
        group(g * ATT_GROUP, True)

    def body(g, carry):
        group(pl.multiple_of(g * ATT_GROUP, ATT_GROUP), False)
        return carry

    n_body = n_groups - n_head
    lax.fori_loop(n_head, n_groups, body, 0, unroll=3 if n_body % 3 == 0 else 1)


def _band_attention(z, bias, *, n_seq, seq_len):
    qb, kb, vb = Z_QA // ATT_DH, Z_KA // ATT_DH, Z_VA // ATT_DH
    kern = functools.partial(_band_kernel, seq_len=seq_len)
    return pl.pallas_call(
        kern,
        grid=(ATT_HEADS, n_seq),
        in_specs=[
            pl.BlockSpec((seq_len, ATT_DH), lambda h, b: (b, qb + h)),
            pl.BlockSpec((seq_len, ATT_DH), lambda h, b: (b, kb + h)),
            pl.BlockSpec((seq_len, ATT_DH), lambda h, b: (b, vb + h)),
            pl.BlockSpec((1, ATT_GROUP, ATT_WINDOW), lambda h, b: (h, 0, 0)),
        ],
        out_specs=pl.BlockSpec((seq_len, ATT_DH), lambda h, b: (b, h)),
        out_shape=jax.ShapeDtypeStruct((n_seq * seq_len, ATT_WIDTH), BF16),
        scratch_shapes=[pltpu.VMEM((seq_len + BAND_PAST, ATT_DH), BF16),
                        pltpu.VMEM((seq_len + BAND_PAST, ATT_DH), BF16)],
        compiler_params=_cparams(2),
        name="band_attention",
    )(z, z, z, bias)


def _band_step_kernel(q_ref, kn_ref, vn_ref, kc_ref, vc_ref, bias_ref, o_ref, *, n_new, n_past):
    scale = ATT_DH ** -0.5
    for h in range(ATT_HEADS):
        lanes = slice(h * ATT_DH, (h + 1) * ATT_DH)
        qh = q_ref[:, lanes]
        kc = kc_ref[0, pl.ds(h, n_past, stride=ATT_HEADS), :].astype(BF16)
        vc = vc_ref[0, pl.ds(h, n_past, stride=ATT_HEADS), :].astype(BF16)
        s_p = _dot_nt(qh, kc) * scale + bias_ref[h, :, 0:n_past]
        s_n = _dot_nt(qh, kn_ref[:, lanes]) * scale + bias_ref[h, :, n_past:n_past + n_new]
        m = jnp.maximum(jnp.max(s_p, axis=-1, keepdims=True), jnp.max(s_n, axis=-1, keepdims=True))
        p_p = jnp.exp(s_p - m)
        p_n = jnp.exp(s_n - m)
        l = jnp.sum(p_p, axis=-1, keepdims=True) + jnp.sum(p_n, axis=-1, keepdims=True)
        o = (_dot(p_p.astype(BF16), vc) + _dot(p_n.astype(BF16), vn_ref[:, lanes])) / l
        o_ref[:, lanes] = o.astype(BF16)


def _band_attention_step(z, cache_k, cache_v, bias, *, n_seq, n_new):
    n_past = cache_k.shape[1]
    kc = cache_k.reshape(n_seq, n_past * ATT_HEADS, ATT_DH)
    vc = cache_v.reshape(n_seq, n_past * ATT_HEADS, ATT_DH)
    qb, kb, vb = Z_QA // ATT_WIDTH, Z_KA // ATT_WIDTH, Z_VA // ATT_WIDTH
    kern = functools.partial(_band_step_kernel, n_new=n_new, n_past=n_past)
    return pl.pallas_call(
        kern,
        grid=(n_seq,),
        in_specs=[
            pl.BlockSpec((n_new, ATT_WIDTH), lambda b: (b, qb)),
            pl.BlockSpec((n_new, ATT_WIDTH), lambda b: (b, kb)),
            pl.BlockSpec((n_new, ATT_WIDTH), lambda b: (b, vb)),
            pl.BlockSpec((1, n_past * ATT_HEADS, ATT_DH), lambda b: (b, 0, 0)),
            pl.BlockSpec((1, n_past * ATT_HEADS, ATT_DH), lambda b: (b, 0, 0)),
            pl.BlockSpec(bias.shape, lambda b: (0, 0, 0)),
        ],
        out_specs=pl.BlockSpec((n_new, ATT_WIDTH), lambda b: (b, 0)),
        out_shape=jax.ShapeDtypeStruct((n_seq * n_new, ATT_WIDTH), BF16),
        compiler_params=_cparams(1),
        name="band_attention_step",
    )(z, z, z, kc, vc, bias)


def _merge_kernel(og_ref, r_ref, oa_ref, mg_ref, ma_ref, x_ref, ng_ref, wg_ref, wa_ref, wo_ref, mb_ref,
                  g_ref, b_ref, h_ref, *, alpha):
    og = og_ref[...].astype(F32)
    normed = []
    for hd in range(GLA_HEADS):
        oh = og[:, hd * GLA_DV:(hd + 1) * GLA_DV]
        ms = jnp.mean(oh * oh, axis=-1, keepdims=True)
        normed.append(oh * lax.rsqrt(ms + RMS_EPS))
    rf = r_ref[...].astype(F32)
    o_gla = (jnp.concatenate(normed, axis=1) * ng_ref[...]) * (rf * _sigmoid(rf))
    gate_g = _sigmoid(mg_ref[...].astype(F32) + mb_ref[0:1, :])
    gate_a = _sigmoid(ma_ref[...].astype(F32) + mb_ref[1:2, :])
    mixed = (gate_g * _dot(o_gla.astype(BF16), wg_ref[...])
             + gate_a * _dot(oa_ref[...], wa_ref[...]))
    mix = _dot(mixed.astype(BF16), wo_ref[...])
    h_ref[...] = _layer_norm(alpha * x_ref[...] + mix, g_ref[...], b_ref[...])


def _merge(o_gla_raw, o_att, z, x2d, norm_g, w_br_gla, w_br_att, w_out, merge_b, ln_g, ln_b, *, alpha):
    m = x2d.shape[0]
    bm = MERGE_BM
    rgb, mgb, mab = Z_RG // D_MODEL, Z_MG // D_MODEL, Z_MA // D_MODEL
    resident = dict(pipeline_mode=pl.Buffered(1))
    kern = functools.partial(_merge_kernel, alpha=alpha)
    return pl.pallas_call(
        kern,
        grid=(m // bm,),
        in_specs=[
            pl.BlockSpec((bm, GLA_V), lambda i: (i, 0)),
            pl.BlockSpec((bm, GLA_V), lambda i: (i, rgb)),
            pl.BlockSpec((bm, ATT_WIDTH), lambda i: (i, 0)),
            pl.BlockSpec((bm, D_MODEL), lambda i: (i, mgb)),
            pl.BlockSpec((bm, D_MODEL), lambda i: (i, mab)),
            pl.BlockSpec((bm, D_MODEL), lambda i: (i, 0)),
            pl.BlockSpec((1, GLA_V), lambda i: (0, 0)),
            pl.BlockSpec((GLA_V, D_MODEL), lambda i: (0, 0), **resident),
            pl.BlockSpec((ATT_WIDTH, D_MODEL), lambda i: (0, 0), **resident),
            pl.BlockSpec((D_MODEL, D_MODEL), lambda i: (0, 0), **resident),
            pl.BlockSpec((2, D_MODEL), lambda i: (0, 0)),
            pl.BlockSpec((1, D_MODEL), lambda i: (0, 0)),
            pl.BlockSpec((1, D_MODEL), lambda i: (0, 0)),
        ],
        out_specs=pl.BlockSpec((bm, D_MODEL), lambda i: (i, 0)),
        out_shape=jax.ShapeDtypeStruct((m, D_MODEL), F32),
        compiler_params=_cparams(1),
        name="merge",
    )(o_gla_raw, z, o_att, z, z, x2d, norm_g, w_br_gla, w_br_att, w_out, merge_b, ln_g, ln_b)


def _ffn_kernel(h_ref, wu_ref, wg_ref, wd_ref, cw_ref, cb_ref, g_ref, b_ref, e0_ref, e1_ref,
                y_ref, ut_ref, hb_ref, carry_ref, *, alpha, seq_len, bm, whole_u):
    i = pl.program_id(0)
    j = pl.program_id(1)

    @pl.when(j == 0)
    def _():
        hb_ref[...] = h_ref[...].astype(BF16)
        y_ref[...] = jnp.zeros(y_ref.shape, F32)

    if seq_len >= bm:
        @pl.when(i == 0)
        def _():
            carry_ref[j] = jnp.zeros(carry_ref.shape[1:], F32)

    n_col = FFN_BF // FFN_HALF
    row = lax.broadcasted_iota(jnp.int32, (FFN_ROWS, FFN_HALF), 0)
    in_stream = seq_len >= bm
    tails = [None] * n_col
    for r in range(bm // FFN_ROWS):
        rows = slice(r * FFN_ROWS, (r + 1) * FFN_ROWS)
        hb = hb_ref[rows, :]
        part = None
        for c in range(n_col):
            cols = slice(c * FFN_HALF, (c + 1) * FFN_HALF)
            u = _dot(hb, wu_ref[:, cols])
            gate = _dot(hb, wg_ref[:, cols])
            if in_stream:
                if r == 0:
                    first = lax.rem(i, seq_len // bm) == 0
                    p0 = jnp.where(first, e0_ref[0, :, cols], carry_ref[j, 6:7, cols])
                    p1 = jnp.where(first, e1_ref[0, :, cols], carry_ref[j, 7:8, cols])
                else:
                    p0, p1 = tails[c][6:7, :], tails[c][7:8, :]
                tails[c] = u[FFN_ROWS - 8:FFN_ROWS, :]
                is0 = row == 0
                is1 = row == 1
            else:
                p0 = e0_ref[rows, cols]
                p1 = e1_ref[rows, cols]
                pos = row & (seq_len - 1)
                is0 = pos == 0
                is1 = pos == 1
            u_m1 = jnp.where(is0, p1, pltpu.roll(u, 1, 0))
            u_m2 = jnp.where(is0, p0, jnp.where(is1, p1, pltpu.roll(u, 2, 0)))
            uc = (cb_ref[:, cols] + u_m2 * cw_ref[0:1, cols] + u_m1 * cw_ref[1:2, cols]
                  + u * cw_ref[2:3, cols])
            gelu = 0.5 * uc * (1.0 + jnp.tanh(math.sqrt(2.0 / math.pi) * (uc + 0.044715 * (uc * uc * uc))))
            d = _dot((gelu * gate).astype(BF16), wd_ref[cols, :])
            part = d if part is None else part + d
            if whole_u:
                ut_ref[rows, cols] = u
        y_ref[rows, :] += part
    if in_stream:
        for c in range(n_col):
            cols = slice(c * FFN_HALF, (c + 1) * FFN_HALF)
            carry_ref[j, :, cols] = tails[c]
            ut_ref[0, :, cols] = tails[c]

    @pl.when(j == pl.num_programs(1) - 1)
    def _():
        y_ref[...] = _layer_norm(alpha * h_ref[...] + y_ref[...], g_ref[...], b_ref[...])


def _ffn(h, w_up_u, w_up_g, w_down, conv_w, conv_b, ln_g, ln_b, conv_prev, *, alpha, n_seq, seq_len):
    m = h.shape[0]
    n_ff = D_FF_PAD // FFN_BF
    bm = FFN_BM if seq_len >= FFN_BM else FFN_BM_STREAMS
    if seq_len >= bm:
        tiles_per_seq = seq_len // bm
        e0 = conv_prev[:, 0:1, :]
        e1 = conv_prev[:, 1:2, :]
        e_spec = pl.BlockSpec((1, 1, FFN_BF), lambda i, j: (i // tiles_per_seq, 0, j))
        ut_shape = jax.ShapeDtypeStruct((m // bm, 8, D_FF_PAD), F32)
        ut_spec = pl.BlockSpec((1, 8, FFN_BF), lambda i, j: (i, 0, j))
        whole_u = False
    else:
        assert bm % seq_len == 0 and seq_len & (seq_len - 1) == 0 and seq_len >= CONV_W - 1
        e0 = jnp.repeat(conv_prev[:, 0, :], seq_len, axis=0)
        e1 = jnp.repeat(conv_prev[:, 1, :], seq_len, axis=0)
        e_spec = pl.BlockSpec((bm, FFN_BF), lambda i, j: (i, j))
        ut_shape = jax.ShapeDtypeStruct((m, D_FF_PAD), F32)
        ut_spec = pl.BlockSpec((bm, FFN_BF), lambda i, j: (i, j))
        whole_u = True
    kern = functools.partial(_ffn_kernel, alpha=alpha, seq_len=seq_len, bm=bm, whole_u=whole_u)
    return pl.pallas_call(
        kern,
        grid=(m // bm, n_ff),
        in_specs=[
            pl.BlockSpec((bm, D_MODEL), lambda i, j: (i, 0)),
            pl.BlockSpec((D_MODEL, FFN_BF), lambda i, j: (0, j)),
            pl.BlockSpec((D_MODEL, FFN_BF), lambda i, j: (0, j)),
            pl.BlockSpec((FFN_BF, D_MODEL), lambda i, j: (j, 0)),
            pl.BlockSpec((CONV_W, FFN_BF), lambda i, j: (0, j)),
            pl.BlockSpec((1, FFN_BF), lambda i, j: (0, j)),
            pl.BlockSpec((1, D_MODEL), lambda i, j: (0, 0)),
            pl.BlockSpec((1, D_MODEL), lambda i, j: (0, 0)),
            e_spec,
            e_spec,
        ],
        out_specs=[pl.BlockSpec((bm, D_MODEL), lambda i, j: (i, 0)), ut_spec],
        out_shape=[jax.ShapeDtypeStruct((m, D_MODEL), F32), ut_shape],
        scratch_shapes=[pltpu.VMEM((bm, D_MODEL), BF16),
                        pltpu.VMEM((n_ff, 8, FFN_BF), F32)],
        compiler_params=_cparams(2),
        name="ffn",
    )(h, w_up_u, w_up_g, w_down, conv_w, conv_b, ln_g, ln_b, e0, e1)


def _relayout_w_in_kernel(wt_ref, alo_t_ref, main_ref, alo_ref):
    main_ref[...] = wt_ref[...].T.astype(BF16)

    @pl.when(pl.program_id(0) == 0)
    def _():
        zeros = jnp.zeros((ALO_PAD - GLA_GATE_RANK, alo_t_ref.shape[1]), F32)
        alo_ref[...] = jnp.concatenate([alo_t_ref[...], zeros], axis=0).T.astype(BF16)


def _relayout_w_in(w_in_t):
    a1 = Z_MG + GLA_GATE_RANK
    n_att = 3 * ATT_WIDTH
    n_head, n_gate = Z_MG // INPROJ_BN, (Z_QA - Z_MG) // INPROJ_BN

    sub = 8
    assert a1 % sub == 0 and n_att % sub == 0 and INPROJ_BN % sub == 0

    def src_row(j):
        gates = (a1 + n_att) // sub + (j - n_head) * (INPROJ_BN // sub)
        att = a1 // sub + (j - n_head - n_gate) * (INPROJ_BN // sub)
        head = j * (INPROJ_BN // sub)
        return jnp.where(j < n_head, head, jnp.where(j < n_head + n_gate, gates, att)) * sub

    return pl.pallas_call(
        _relayout_w_in_kernel,
        grid=(Z_WIDTH // INPROJ_BN,),
        in_specs=[pl.BlockSpec((pl.Element(INPROJ_BN), pl.Element(D_MODEL)), lambda j: (src_row(j), 0)),
                  pl.BlockSpec((pl.Element(GLA_GATE_RANK), pl.Element(D_MODEL)), lambda j: (Z_MG, 0))],
        out_specs=[pl.BlockSpec((D_MODEL, INPROJ_BN), lambda j: (0, j)),
                   pl.BlockSpec((D_MODEL, ALO_PAD), lambda j: (0, 0))],
        out_shape=[jax.ShapeDtypeStruct((D_MODEL, Z_WIDTH), BF16),
                   jax.ShapeDtypeStruct((D_MODEL, ALO_PAD), BF16)],
        compiler_params=_cparams(1),
        name="relayout_w_in",
    )(w_in_t, w_in_t)


def _split_w_up_kernel(w_ref, u_ref, g_ref):
    pad = jnp.zeros((w_ref.shape[0], D_FF_PAD - D_FF), BF16)
    u_ref[:, 0:D_FF] = w_ref[:, 0:D_FF].astype(BF16)
    u_ref[:, D_FF:D_FF_PAD] = pad
    g_ref[:, 0:D_FF] = w_ref[:, D_FF:2 * D_FF].astype(BF16)
    g_ref[:, D_FF:D_FF_PAD] = pad


def _split_w_up(w_ffn_up, layer):
    k = w_ffn_up.shape[1]
    half = pl.BlockSpec((PREP_ROWS, D_FF_PAD), lambda i: (i, 0))
    return pl.pallas_call(
        _split_w_up_kernel,
        grid=(k // PREP_ROWS,),
        in_specs=[pl.BlockSpec((None, PREP_ROWS, 2 * D_FF), lambda i: (layer, i, 0))],
        out_specs=[half, half],
        out_shape=[jax.ShapeDtypeStruct((k, D_FF_PAD), BF16)] * 2,
        compiler_params=_cparams(1),
        name="split_w_up",
    )(w_ffn_up)


def _prepare_params(layer, w_in_all, w_ffn_up_all, gla_gate_up, gla_gate_b, gla_norm_g, w_br_gla, w_br_att,
                    w_out, ln1_g, ln1_b, ffn_conv_w, ffn_conv_b, w_ffn_down, ln2_g, ln2_b):
    w_main, w_alo = _relayout_w_in(w_in_all[layer].T)
    w_up_u, w_up_g = _split_w_up(w_ffn_up_all, layer)
    gup = jnp.pad(gla_gate_up, ((0, ALO_PAD - GLA_GATE_RANK), (0, 0))).astype(BF16)
    ffpad = D_FF_PAD - D_FF
    return dict(
        w_main=w_main, w_alo=w_alo, gup=gup,
        gate_b=gla_gate_b.reshape(1, GLA_QK), norm_g=gla_norm_g.reshape(1, GLA_V),
        w_br_gla=w_br_gla.astype(BF16), w_br_att=w_br_att.astype(BF16), w_out=w_out.astype(BF16),
        ln1_g=ln1_g.reshape(1, D_MODEL), ln1_b=ln1_b.reshape(1, D_MODEL),
        w_up_u=w_up_u, w_up_g=w_up_g,
        w_down=jnp.pad(w_ffn_down, ((0, ffpad), (0, 0))).astype(BF16),
        conv_w=jnp.pad(ffn_conv_w, ((0, 0), (0, ffpad))),
        conv_b=jnp.pad(ffn_conv_b.reshape(1, D_FF), ((0, 0), (0, ffpad))),
        ln2_g=ln2_g.reshape(1, D_MODEL), ln2_b=ln2_b.reshape(1, D_MODEL),
    )


def _layer(x, past_k, past_v, s0, conv_prev, p, table, merge_b, *, alpha):
    n_seq, seq_len, _ = x.shape
    m = n_seq * seq_len
    x2d = x.reshape(m, D_MODEL)
    z, alo, k_new, v_new = _inproj(x2d, p["w_main"], p["w_alo"], seq_len=seq_len)
    o_gla, s_fin = _gla(z, alo, p["gup"], p["gate_b"], s0, n_seq=n_seq, seq_len=seq_len)
    keep = min(BAND_PAST, seq_len)
    k_new = k_new.reshape(n_seq, keep, ATT_HEADS, ATT_DH)
    v_new = v_new.reshape(n_seq, keep, ATT_HEADS, ATT_DH)
    if past_k is None:
        bias = _rel_bias(table, n_q=ATT_GROUP, n_k=ATT_WINDOW, off=BAND_PAST, band=True)
        o_att = _band_attention(z, bias, n_seq=n_seq, seq_len=seq_len)
    else:
        n_past = past_k.shape[1]
        n_k = ((n_past + seq_len + V7X_LANES - 1) // V7X_LANES) * V7X_LANES
        bias = _rel_bias(table, n_q=seq_len, n_k=n_k, off=n_past, band=False)
        o_att = _band_attention_step(z, past_k, past_v, bias, n_seq=n_seq, n_new=seq_len)
    h = _merge(o_gla, o_att, z, x2d, p["norm_g"], p["w_br_gla"], p["w_br_att"], p["w_out"], merge_b,
               p["ln1_g"], p["ln1_b"], alpha=alpha)
    prev = jnp.pad(conv_prev, ((0, 0), (0, 0), (0, D_FF_PAD - D_FF)))
    y, ut = _ffn(h, p["w_up_u"], p["w_up_g"], p["w_down"], p["conv_w"], p["conv_b"],
                 p["ln2_g"], p["ln2_b"], prev, alpha=alpha, n_seq=n_seq, seq_len=seq_len)
    if seq_len >= FFN_BM:
        tiles_per_seq = seq_len // FFN_BM
        conv_new = ut.reshape(n_seq, tiles_per_seq, 8, D_FF_PAD)[:, -1, 8 - (CONV_W - 1):, :D_FF]
    else:
        conv_new = ut.reshape(n_seq, seq_len, D_FF_PAD)[:, seq_len - (CONV_W - 1):, :D_FF]
    return (y.reshape(n_seq, seq_len, D_MODEL), k_new, v_new, s_fin, conv_new)


def kernel(x_prompt, x_sample, cache_att_k, cache_att_v, state_gla, state_ffn_conv, w_in, gla_gate_up,
           gla_gate_b, gla_norm_g, att_rel_bias, merge_b, w_br_gla, w_br_att, w_out, ln1_g, ln1_b,
           w_ffn_up, ffn_conv_w, ffn_conv_b, w_ffn_down, ln2_g, ln2_b):
    depth = w_in.shape[0]
    alpha = (2.0 * depth) ** 0.25
    xp, xs = x_prompt, x_sample
    n_p = xp.shape[0]
    outs_p, outs_s = [], []
    for l in range(depth):
        p = _prepare_params(l, w_in, w_ffn_up, gla_gate_up[l], gla_gate_b[l], gla_norm_g[l], w_br_gla[l],
                            w_br_att[l], w_out[l], ln1_g[l], ln1_b[l], ffn_conv_w[l],
                            ffn_conv_b[l], w_ffn_down[l], ln2_g[l], ln2_b[l])
        s0_p = jnp.zeros((n_p, GLA_HEADS, GLA_DK, GLA_DV), F32)
        conv0_p = jnp.zeros((n_p, CONV_W - 1, D_FF), F32)
        xp, *rest_p = _layer(xp, None, None, s0_p, conv0_p, p, att_rel_bias[l], merge_b[l], alpha=alpha)
        outs_p.append(rest_p)
        xs, *rest_s = _layer(xs, cache_att_k[l], cache_att_v[l], state_gla[l], state_ffn_conv[l], p,
                             att_rel_bias[l], merge_b[l], alpha=alpha)
        outs_s.append(rest_s)
    stack = lambda outs, idx: jnp.stack([o[idx] for o in outs])
    return (xp, xs,
            stack(outs_p, 0), stack(outs_p, 1), stack(outs_p, 2), stack(outs_p, 3),
            stack(outs_s, 0), stack(outs_s, 1), stack(outs_s, 2), stack(outs_s, 3))
```

```python
import functools
import math

import jax
import jax.numpy as jnp
from jax import lax
from jax.experimental import pallas as pl
from jax.experimental.pallas import tpu as pltpu

F32 = jnp.float32
BF16 = jnp.bfloat16

D_MODEL = 2048
CHUNK = 64
CHUNK_SHIFT = CHUNK.bit_length() - 1
assert 1 << CHUNK_SHIFT == CHUNK
GLA_HEADS = 4
GLA_QK = D_MODEL // 2
GLA_V = D_MODEL
GLA_DK = GLA_QK // GLA_HEADS
GLA_DV = GLA_V // GLA_HEADS
GLA_GATE_RANK = 16
GLA_TAU = 16.0
ATT_HEADS = 8
ATT_DH = 128
ATT_WIDTH = ATT_HEADS * ATT_DH
BAND_CHUNKS = 8
BAND_PAST = BAND_CHUNKS * CHUNK
REL_CLIP = 128
D_FF = 5504
CONV_W = 3
LN_EPS = 1e-5
RMS_EPS = 1e-6
NEG_INF = -1e30

V7X_LANES = 128
V7X_VMEM_BYTES = 64 * 1024 * 1024
VMEM_LIMIT = V7X_VMEM_BYTES - 4 * 1024 * 1024

Z_QG = 0
Z_KG = Z_QG + GLA_QK
Z_VG = Z_KG + GLA_QK
Z_RG = Z_VG + GLA_V
Z_MG = Z_RG + GLA_V
Z_MA = Z_MG + D_MODEL
Z_QA = Z_MA + D_MODEL
Z_KA = Z_QA + ATT_WIDTH
Z_VA = Z_KA + ATT_WIDTH
Z_WIDTH = Z_VA + ATT_WIDTH
ALO_PAD = V7X_LANES

INPROJ_BM = 1024
INPROJ_BN = 1024
MERGE_BM = 256
PREP_ROWS = 256
GLA_ROWS = 512
GLA_SUPER = 4 * CHUNK
FFN_BM = 1024
FFN_BM_STREAMS = 512
FFN_BF = 512
FFN_HALF = 512
FFN_ROWS = 512
D_FF_PAD = ((D_FF + FFN_BF - 1) // FFN_BF) * FFN_BF
ATT_GROUP = 4 * CHUNK
ATT_WINDOW = ATT_GROUP + BAND_PAST


def _cparams(n_axes):
    return pltpu.CompilerParams(
        dimension_semantics=("arbitrary",) * n_axes, vmem_limit_bytes=VMEM_LIMIT)


def _dot(a, b):
    return jnp.dot(a, b, preferred_element_type=F32)


def _dot_nt(a, b):
    return lax.dot_general(a, b, (((1,), (1,)), ((), ())), preferred_element_type=F32)


def _dot_tn(a, b):
    return lax.dot_general(a, b, (((0,), (0,)), ((), ())), preferred_element_type=F32)


LOG2_E = 1.4426950408889634


def _top_bits(x):
    bits = lax.bitcast_convert_type(x, jnp.uint32) & jnp.uint32(0xFFFF0000)
    return lax.bitcast_convert_type(bits, F32)


def _sigmoid(x):
    return 1.0 / (1.0 + jnp.exp(-x))


def _layer_norm(x, g, b):
    mu = jnp.mean(x, axis=-1, keepdims=True)
    xc = x - mu
    var = jnp.mean(xc * xc, axis=-1, keepdims=True)
    return xc * lax.rsqrt(var + LN_EPS) * g + b


def _inproj_kernel(x_ref, w_ref, walo_ref, z_ref, alo_ref, k32_ref, v32_ref, xb_ref, *,
                   keep_from, keep_rows, keep_every):
    i = pl.program_id(0)
    j = pl.program_id(1)

    @pl.when(j == 0)
    def _():
        xb_ref[...] = x_ref[...].astype(BF16)
        alo_ref[...] = _dot(xb_ref[...], walo_ref[...])

    res = _dot(xb_ref[...], w_ref[...])
    z_ref[...] = res.astype(BF16)

    keep_tile = lax.rem(i, keep_every) == keep_every - 1

    @pl.when(keep_tile & (j == Z_KA // INPROJ_BN))
    def _():
        k32_ref[...] = res[keep_from:keep_from + keep_rows, :]

    @pl.when(keep_tile & (j == Z_VA // INPROJ_BN))
    def _():
        v32_ref[...] = res[keep_from:keep_from + keep_rows, :]


def _inproj(x2d, w_main, w_alo, *, seq_len):
    assert INPROJ_BN == ATT_WIDTH and Z_KA % INPROJ_BN == 0 and Z_VA % INPROJ_BN == 0
    m = x2d.shape[0]
    bm = min(INPROJ_BM, m)
    keep = min(BAND_PAST, seq_len)
    if seq_len >= bm:
        assert seq_len % bm == 0 and keep <= bm
        keep_from, keep_rows, keep_every = bm - keep, keep, seq_len // bm
    else:
        assert keep == seq_len and bm % seq_len == 0
        keep_from, keep_rows, keep_every = 0, bm, 1
    n_keep = m // bm // keep_every * keep_rows
    kern = functools.partial(_inproj_kernel, keep_from=keep_from, keep_rows=keep_rows,
                             keep_every=keep_every)
    keep_spec = pl.BlockSpec((keep_rows, ATT_WIDTH), lambda i, j: (i // keep_every, 0))
    return pl.pallas_call(
        kern,
        grid=(m // bm, Z_WIDTH // INPROJ_BN),
        in_specs=[
            pl.BlockSpec((bm, D_MODEL), lambda i, j: (i, 0)),
            pl.BlockSpec((D_MODEL, INPROJ_BN), lambda i, j: (0, j)),
            pl.BlockSpec((D_MODEL, ALO_PAD), lambda i, j: (0, 0)),
        ],
        out_specs=[
            pl.BlockSpec((bm, INPROJ_BN), lambda i, j: (i, j)),
            pl.BlockSpec((bm, ALO_PAD), lambda i, j: (i, 0)),
            keep_spec,
            keep_spec,
        ],
        out_shape=[
            jax.ShapeDtypeStruct((m, Z_WIDTH), BF16),
            jax.ShapeDtypeStruct((m, ALO_PAD), F32),
            jax.ShapeDtypeStruct((n_keep, ATT_WIDTH), F32),
            jax.ShapeDtypeStruct((n_keep, ATT_WIDTH), F32),
        ],
        scratch_shapes=[pltpu.VMEM((bm, D_MODEL), BF16)],
        compiler_params=_cparams(2),
        name="inproj",
    )(x2d, w_main, w_alo)


def _gla_kernel(q_ref, k_ref, v_ref, alo_ref, gup_ref, gb_ref, s0_ref, o_ref, sfin_ref, st_ref, *,
                rows_per_step, sb, blk):
    t = pl.program_id(1)
    n_sub = sb // blk
    shift = blk.bit_length() - 1
    assert 1 << shift == blk

    @pl.when(t == 0)
    def _():
        for h in range(GLA_HEADS):
            st_ref[h] = s0_ref[0, h].T

    row = lax.broadcasted_iota(jnp.int32, (sb, sb), 0)
    col = lax.broadcasted_iota(jnp.int32, (sb, sb), 1)
    dblk = (row >> shift) - (col >> shift)
    same_blk_causal = (dblk == 0) & (row >= col)
    later_blk = {d: dblk == d for d in range(1, n_sub)}
    tril = jnp.where(same_blk_causal, 1.0, 0.0).astype(BF16)

    def rows_of(blocks):
        return jnp.concatenate([jnp.broadcast_to(r, (blk, GLA_QK)) for r in blocks], axis=0)

    def body(s, carry):
        rows = pl.ds(pl.multiple_of(s * sb, sb), sb)
        x = _dot(alo_ref[rows, :].astype(BF16), gup_ref[...]) + gb_ref[...]
        log_a = (jnp.minimum(x, 0.0) - jnp.log(1.0 + jnp.exp2(jnp.abs(x) * -LOG2_E))) * (1.0 / GLA_TAU)
        hi = _top_bits(log_a)
        rem = log_a - hi
        mid = _top_bits(rem)
        lo = rem - mid
        b = _dot(tril, hi.astype(BF16)) + _dot(tril, mid.astype(BF16)) + _dot(tril, lo.astype(BF16))
        blk_sum = [b[(j + 1) * blk - 1:(j + 1) * blk, :] for j in range(n_sub)]
        cum = [blk_sum[0]]
        for j in range(1, n_sub):
            cum.append(cum[-1] + blk_sum[j])
        total = cum[-1]
        qf = q_ref[rows, :].astype(F32) * (GLA_DK ** -0.5)
        kf = k_ref[rows, :].astype(F32)
        q_tf = qf * jnp.exp(b)
        k_decf = kf * jnp.exp(rows_of(blk_sum) - b)
        q_t = q_tf.astype(BF16)
        k_t = (kf * jnp.exp(-b)).astype(BF16)
        k_dec = k_decf.astype(BF16)
        if n_sub > 1:
            one = jnp.ones((1, GLA_QK), F32)
            q_s = (q_tf * rows_of([one] + [jnp.exp(cum[j - 1]) for j in range(1, n_sub)])).astype(BF16)
            k_s = (k_decf * rows_of([jnp.exp(total - cum[j]) for j in range(n_sub)])).astype(BF16)
            k_far = {1: k_dec}
            zero = jnp.zeros((1, GLA_QK), F32)
            for d in range(2, n_sub):
                scale = [jnp.exp(cum[j + d - 1] - cum[j]) if j + d < n_sub else zero for j in range(n_sub)]
                k_far[d] = (k_decf * rows_of(scale)).astype(BF16)
        else:
            q_s, k_s, k_far = q_t, k_dec, {}
        decay = jnp.exp(total)
        for h in range(GLA_HEADS):
            lk = slice(h * GLA_DK, (h + 1) * GLA_DK)
            lv = slice(h * GLA_DV, (h + 1) * GLA_DV)
            att = jnp.where(same_blk_causal, _dot_nt(q_t[:, lk], k_t[:, lk]), 0.0)
            for d, kd in k_far.items():
                att = jnp.where(later_blk[d], _dot_nt(q_t[:, lk], kd[:, lk]), att)
            vb = v_ref[rows, lv]
            st = st_ref[h]
            o = _dot(att.astype(BF16), vb) + _dot_nt(q_s[:, lk], st.astype(BF16))
            st_ref[h] = st * decay[:, lk] + _dot_tn(vb, k_s[:, lk])
            o_ref[rows, lv] = o.astype(BF16)
        return carry

    n_sb = rows_per_step // sb
    lax.fori_loop(0, n_sb, body, 0, unroll=2 if n_sb % 2 == 0 else 1)

    @pl.when(t == pl.num_programs(1) - 1)
    def _():
        for h in range(GLA_HEADS):
            sfin_ref[0, h] = st_ref[h].T


def _gla(z, alo, gup_pad, gate_b, s0, *, n_seq, seq_len):
    blk = min(CHUNK, seq_len)
    sb = min(GLA_SUPER, seq_len)
    rt = min(GLA_ROWS, seq_len)
    n_t = seq_len // rt
    kern = functools.partial(_gla_kernel, rows_per_step=rt, sb=sb, blk=blk)
    row_block = lambda b, t: b * n_t + t
    return pl.pallas_call(
        kern,
        grid=(n_seq, n_t),
        in_specs=[
            pl.BlockSpec((rt, GLA_QK), lambda b, t: (row_block(b, t), Z_QG // GLA_QK)),
            pl.BlockSpec((rt, GLA_QK), lambda b, t: (row_block(b, t), Z_KG // GLA_QK)),
            pl.BlockSpec((rt, GLA_V), lambda b, t: (row_block(b, t), Z_VG // GLA_V)),
            pl.BlockSpec((rt, ALO_PAD), lambda b, t: (row_block(b, t), 0)),
            pl.BlockSpec((ALO_PAD, GLA_QK), lambda b, t: (0, 0)),
            pl.BlockSpec((1, GLA_QK), lambda b, t: (0, 0)),
            pl.BlockSpec((1, GLA_HEADS, GLA_DK, GLA_DV), lambda b, t: (b, 0, 0, 0)),
        ],
        out_specs=[
            pl.BlockSpec((rt, GLA_V), lambda b, t: (row_block(b, t), 0)),
            pl.BlockSpec((1, GLA_HEADS, GLA_DK, GLA_DV), lambda b, t: (b, 0, 0, 0)),
        ],
        out_shape=[
            jax.ShapeDtypeStruct((n_seq * seq_len, GLA_V), BF16),
            jax.ShapeDtypeStruct((n_seq, GLA_HEADS, GLA_DK, GLA_DV), F32),
        ],
        scratch_shapes=[pltpu.VMEM((GLA_HEADS, GLA_DV, GLA_DK), F32)],
        compiler_params=_cparams(2),
        name="gla",
    )(z, z, z, alo, gup_pad, gate_b, s0)


def _rel_bias_kernel(tab_ref, out_ref, *, n_q, n_k, off, band):
    h = pl.program_id(0)
    k8 = lax.broadcasted_iota(jnp.int32, (8, n_k), 1)
    rel0 = jnp.clip(off - k8, -REL_CLIP, REL_CLIP) + REL_CLIP

    def body(r, acc):
        return jnp.where(rel0 == r, tab_ref[h, r], acc)

    row0 = lax.fori_loop(0, 2 * REL_CLIP + 1, body, jnp.zeros((8, n_k), F32))
    full = jnp.broadcast_to(row0[0:1, :], (n_q, n_k))
    rolled = pltpu.roll(full, 0, 1, stride=1, stride_axis=0)
    q = lax.broadcasted_iota(jnp.int32, (n_q, n_k), 0)
    k = lax.broadcasted_iota(jnp.int32, (n_q, n_k), 1)
    bias = jnp.where(k < q, tab_ref[h, 2 * REL_CLIP], rolled)
    if band:
        dc = (k >> CHUNK_SHIFT) - (q >> CHUNK_SHIFT)
        bias = jnp.where(dc < 0, NEG_INF, jnp.where(dc > BAND_CHUNKS, NEG_INF, bias))
    out_ref[0] = bias


def _rel_bias(table, *, n_q, n_k, off, band):
    assert off >= REL_CLIP and n_k % V7X_LANES == 0
    kern = functools.partial(_rel_bias_kernel, n_q=n_q, n_k=n_k, off=off, band=band)
    return pl.pallas_call(
        kern,
        grid=(ATT_HEADS,),
        in_specs=[pl.BlockSpec(memory_space=pltpu.SMEM)],
        out_specs=pl.BlockSpec((1, n_q, n_k), lambda h: (h, 0, 0)),
        out_shape=jax.ShapeDtypeStruct((ATT_HEADS, n_q, n_k), F32),
        compiler_params=_cparams(1),
        name="rel_bias",
    )(table)


def _band_kernel(q_ref, k_ref, v_ref, bias_ref, o_ref, kp_ref, vp_ref, s_ref, p_ref, l_ref, *, seq_len):
    zeros = jnp.zeros((BAND_PAST, ATT_DH), BF16)
    kp_ref[0:BAND_PAST, :] = zeros
    vp_ref[0:BAND_PAST, :] = zeros
    kp_ref[BAND_PAST:, :] = k_ref[...]
    vp_ref[BAND_PAST:, :] = v_ref[...]
    kk = lax.broadcasted_iota(jnp.int32, (ATT_GROUP, ATT_WINDOW), 1)
    n_groups = seq_len // ATT_GROUP
    n_head = -(-BAND_PAST // ATT_GROUP)
    assert n_groups % 2 == 0 and n_groups >= 4 and n_head <= 2

    def start(g):
        return g * ATT_GROUP if isinstance(g, int) else pl.multiple_of(g * ATT_GROUP, ATT_GROUP)

    def scores(g, slot):
        q0 = start(g)
        s = _dot_nt(q_ref[pl.ds(q0, ATT_GROUP), :], kp_ref[pl.ds(q0, ATT_WINDOW), :])
        s = s * (ATT_DH ** -0.5) + bias_ref[0]
        if isinstance(g, int) and g < n_head:
            s = jnp.where(kk >= BAND_PAST - q0, s, NEG_INF)
        s_ref[slot] = s

    def softmax(slot):
        s = s_ref[slot]
        p = jnp.exp(s - jnp.max(s, axis=-1, keepdims=True))
        l_ref[slot] = jnp.broadcast_to(jnp.sum(p, axis=-1, keepdims=True), l_ref.shape[1:])
        p_ref[slot] = p.astype(BF16)

    def weighted_sum(g, slot):
        q0 = start(g)
        o = _dot(p_ref[slot], vp_ref[pl.ds(q0, ATT_WINDOW), :]) / l_ref[slot]
        o_ref[pl.ds(q0, ATT_GROUP), :] = o.astype(BF16)

    scores(0, 0)
    scores(1, 1)
    softmax(0)

    def body(i, carry):
        g = 2 * i
        scores(g, 0)
        softmax(1)
        weighted_sum(g - 2, 0)
        scores(g + 1, 1)
        softmax(0)
        weighted_sum(g - 1, 1)
        return carry

    lax.fori_loop(1, n_groups // 2, body, 0)
    softmax(1)
    weighted_sum(n_groups - 2, 0)
    weighted_sum(n_groups - 1, 1)


def _band_attention(z, bias, *, n_seq, seq_len):
    qb, kb, vb = Z_QA // ATT_DH, Z_KA // ATT_DH, Z_VA // ATT_DH
    kern = functools.partial(_band_kernel, seq_len=seq_len)
    return pl.pallas_call(
        kern,
        grid=(ATT_HEADS, n_seq),
        in_specs=[
            pl.BlockSpec((seq_len, ATT_DH), lambda h, b: (b, qb + h)),
            pl.BlockSpec((seq_len, ATT_DH), lambda h, b: (b, kb + h)),
            pl.BlockSpec((seq_len, ATT_DH), lambda h, b: (b, vb + h)),
            pl.BlockSpec((1, ATT_GROUP, ATT_WINDOW), lambda h, b: (h, 0, 0)),
        ],
        out_specs=pl.BlockSpec((seq_len, ATT_DH), lambda h, b: (b, h)),
        out_shape=jax.ShapeDtypeStruct((n_seq * seq_len, ATT_WIDTH), BF16),
        scratch_shapes=[pltpu.VMEM((seq_len + BAND_PAST, ATT_DH), BF16),
                        pltpu.VMEM((seq_len + BAND_PAST, ATT_DH), BF16),
                        pltpu.VMEM((2, ATT_GROUP, ATT_WINDOW), F32),
                        pltpu.VMEM((2, ATT_GROUP, ATT_WINDOW), BF16),
                        pltpu.VMEM((2, ATT_GROUP, ATT_DH), F32)],
        compiler_params=_cparams(2),
        name="band_attention",
    )(z, z, z, bias)


def _band_step_kernel(q_ref, kn_ref, vn_ref, kc_ref, vc_ref, bias_ref, o_ref, *, n_new, n_past):
    scale = ATT_DH ** -0.5
    for h in range(ATT_HEADS):
        lanes = slice(h * ATT_DH, (h + 1) * ATT_DH)
        qh = q_ref[:, lanes]
        kc = kc_ref[0, pl.ds(h, n_past, stride=ATT_HEADS), :].astype(BF16)
        vc = vc_ref[0, pl.ds(h, n_past, stride=ATT_HEADS), :].astype(BF16)
        s_p = _dot_nt(qh, kc) * scale + bias_ref[h, :, 0:n_past]
        s_n = _dot_nt(qh, kn_ref[:, lanes]) * scale + bias_ref[h, :, n_past:n_past + n_new]
        m = jnp.maximum(jnp.max(s_p, axis=-1, keepdims=True), jnp.max(s_n, axis=-1, keepdims=True))
        p_p = jnp.exp(s_p - m)
        p_n = jnp.exp(s_n - m)
        l = jnp.sum(p_p, axis=-1, keepdims=True) + jnp.sum(p_n, axis=-1, keepdims=True)
        o = (_dot(p_p.astype(BF16), vc) + _dot(p_n.astype(BF16), vn_ref[:, lanes])) / l
        o_ref[:, lanes] = o.astype(BF16)


def _band_attention_step(z, cache_k, cache_v, bias, *, n_seq, n_new):
    n_past = cache_k.shape[1]
    kc = cache_k.reshape(n_seq, n_past * ATT_HEADS, ATT_DH)
    vc = cache_v.reshape(n_seq, n_past * ATT_HEADS, ATT_DH)
    qb, kb, vb = Z_QA // ATT_WIDTH, Z_KA // ATT_WIDTH, Z_VA // ATT_WIDTH
    kern = functools.partial(_band_step_kernel, n_new=n_new, n_past=n_past)
    return pl.pallas_call(
        kern,
        grid=(n_seq,),
        in_specs=[
            pl.BlockSpec((n_new, ATT_WIDTH), lambda b: (b, qb)),
            pl.BlockSpec((n_new, ATT_WIDTH), lambda b: (b, kb)),
            pl.BlockSpec((n_new, ATT_WIDTH), lambda b: (b, vb)),
            pl.BlockSpec((1, n_past * ATT_HEADS, ATT_DH), lambda b: (b, 0, 0)),
            pl.BlockSpec((1, n_past * ATT_HEADS, ATT_DH), lambda b: (b, 0, 0)),
            pl.BlockSpec(bias.shape, lambda b: (0, 0, 0)),
        ],
        out_specs=pl.BlockSpec((n_new, ATT_WIDTH), lambda b: (b, 0)),
        out_shape=jax.ShapeDtypeStruct((n_seq * n_new, ATT_WIDTH), BF16),
        compiler_params=_cparams(1),
        name="band_attention_step",
    )(z, z, z, kc, vc, bias)


def _merge_kernel(og_ref, r_ref, oa_ref, mg_ref, ma_ref, x_ref, ng_ref, wg_ref, wa_ref, wo_ref, mb_ref,
                  g_ref, b_ref, h_ref, *, alpha):
    og = og_ref[...].astype(F32)
    normed = []
    for hd in range(GLA_HEADS):
        oh = og[:, hd * GLA_DV:(hd + 1) * GLA_DV]
        ms = jnp.mean(oh * oh, axis=-1, keepdims=True)
        normed.append(oh * lax.rsqrt(ms + RMS_EPS))
    rf = r_ref[...].astype(F32)
    o_gla = (jnp.concatenate(normed, axis=1) * ng_ref[...]) * (rf * _sigmoid(rf))
    gate_g = _sigmoid(mg_ref[...].astype(F32) + mb_ref[0:1, :])
    gate_a = _sigmoid(ma_ref[...].astype(F32) + mb_ref[1:2, :])
    mixed = (gate_g * _dot(o_gla.astype(BF16), wg_ref[...])
             + gate_a * _dot(oa_ref[...], wa_ref[...]))
    mix = _dot(mixed.astype(BF16), wo_ref[...])
    h_ref[...] = _layer_norm(alpha * x_ref[...] + mix, g_ref[...], b_ref[...])


def _merge(o_gla_raw, o_att, z, x2d, norm_g, w_br_gla, w_br_att, w_out, merge_b, ln_g, ln_b, *, alpha):
    m = x2d.shape[0]
    bm = MERGE_BM
    rgb, mgb, mab = Z_RG // D_MODEL, Z_MG // D_MODEL, Z_MA // D_MODEL
    resident = dict(pipeline_mode=pl.Buffered(1))
    kern = functools.partial(_merge_kernel, alpha=alpha)
    return pl.pallas_call(
        kern,
        grid=(m // bm,),
        in_specs=[
            pl.BlockSpec((bm, GLA_V), lambda i: (i, 0)),
            pl.BlockSpec((bm, GLA_V), lambda i: (i, rgb)),
            pl.BlockSpec((bm, ATT_WIDTH), lambda i: (i, 0)),
            pl.BlockSpec((bm, D_MODEL), lambda i: (i, mgb)),
            pl.BlockSpec((bm, D_MODEL), lambda i: (i, mab)),
            pl.BlockSpec((bm, D_MODEL), lambda i: (i, 0)),
            pl.BlockSpec((1, GLA_V), lambda i: (0, 0)),
            pl.BlockSpec((GLA_V, D_MODEL), lambda i: (0, 0), **resident),
            pl.BlockSpec((ATT_WIDTH, D_MODEL), lambda i: (0, 0), **resident),
            pl.BlockSpec((D_MODEL, D_MODEL), lambda i: (0, 0), **resident),
            pl.BlockSpec((2, D_MODEL), lambda i: (0, 0)),
            pl.BlockSpec((1, D_MODEL), lambda i: (0, 0)),
            pl.BlockSpec((1, D_MODEL), lambda i: (0, 0)),
        ],
        out_specs=pl.BlockSpec((bm, D_MODEL), lambda i: (i, 0)),
        out_shape=jax.ShapeDtypeStruct((m, D_MODEL), F32),
        compiler_params=_cparams(1),
        name="merge",
    )(o_gla_raw, z, o_att, z, z, x2d, norm_g, w_br_gla, w_br_att, w_out, merge_b, ln_g, ln_b)


def _ffn_kernel(h_ref, wu_ref, wg_ref, wd_ref, cw_ref, cb_ref, g_ref, b_ref, e0_ref, e1_ref,
                y_ref, ut_ref, hb_ref, carry_ref, *, alpha, seq_len, bm, whole_u):
    i = pl.program_id(0)
    j = pl.program_id(1)

    @pl.when(j == 0)
    def _():
        hb_ref[...] = h_ref[...].astype(BF16)
        y_ref[...] = jnp.zeros(y_ref.shape, F32)

    if seq_len >= bm:
        @pl.when(i == 0)
        def _():
            carry_ref[j] = jnp.zeros(carry_ref.shape[1:], F32)

    n_col = FFN_BF // FFN_HALF
    row = lax.broadcasted_iota(jnp.int32, (FFN_ROWS, FFN_HALF), 0)
    in_stream = seq_len >= bm
    tails = [None] * n_col
    for r in range(bm // FFN_ROWS):
        rows = slice(r * FFN_ROWS, (r + 1) * FFN_ROWS)
        hb = hb_ref[rows, :]
        part = None
        for c in range(n_col):
            cols = slice(c * FFN_HALF, (c + 1) * FFN_HALF)
            u = _dot(hb, wu_ref[:, cols])
            gate = _dot(hb, wg_ref[:, cols])
            if in_stream:
                if r == 0:
                    first = lax.rem(i, seq_len // bm) == 0
                    p0 = jnp.where(first, e0_ref[0, :, cols], carry_ref[j, 6:7, cols])
                    p1 = jnp.where(first, e1_ref[0, :, cols], carry_ref[j, 7:8, cols])
                else:
                    p0, p1 = tails[c][6:7, :], tails[c][7:8, :]
                tails[c] = u[FFN_ROWS - 8:FFN_ROWS, :]
                is0 = row == 0
                is1 = row == 1
            else:
                p0 = e0_ref[rows, cols]
                p1 = e1_ref[rows, cols]
                pos = row & (seq_len - 1)
                is0 = pos == 0
                is1 = pos == 1
            u_m1 = jnp.where(is0, p1, pltpu.roll(u, 1, 0))
            u_m2 = jnp.where(is0, p0, jnp.where(is1, p1, pltpu.roll(u, 2, 0)))
            uc = (cb_ref[:, cols] + u_m2 * cw_ref[0:1, cols] + u_m1 * cw_ref[1:2, cols]
                  + u * cw_ref[2:3, cols])
            gelu = 0.5 * uc * (1.0 + jnp.tanh(math.sqrt(2.0 / math.pi) * (uc + 0.044715 * (uc * uc * uc))))
            d = _dot((gelu * gate).astype(BF16), wd_ref[cols, :])
            part = d if part is None else part + d
            if whole_u:
                ut_ref[rows, cols] = u
        y_ref[rows, :] += part
    if in_stream:
        for c in range(n_col):
            cols = slice(c * FFN_HALF, (c + 1) * FFN_HALF)
            carry_ref[j, :, cols] = tails[c]
            ut_ref[0, :, cols] = tails[c]

    @pl.when(j == pl.num_programs(1) - 1)
    def _():
        y_ref[...] = _layer_norm(alpha * h_ref[...] + y_ref[...], g_ref[...], b_ref[...])


def _ffn(h, w_up_u, w_up_g, w_down, conv_w, conv_b, ln_g, ln_b, conv_prev, *, alpha, n_seq, seq_len):
    m = h.shape[0]
    n_ff = D_FF_PAD // FFN_BF
    bm = FFN_BM if seq_len >= FFN_BM else FFN_BM_STREAMS
    if seq_len >= bm:
        tiles_per_seq = seq_len // bm
        e0 = conv_prev[:, 0:1, :]
        e1 = conv_prev[:, 1:2, :]
        e_spec = pl.BlockSpec((1, 1, FFN_BF), lambda i, j: (i // tiles_per_seq, 0, j))
        ut_shape = jax.ShapeDtypeStruct((m // bm, 8, D_FF_PAD), F32)
        ut_spec = pl.BlockSpec((1, 8, FFN_BF), lambda i, j: (i, 0, j))
        whole_u = False
    else:
        assert bm % seq_len == 0 and seq_len & (seq_len - 1) == 0 and seq_len >= CONV_W - 1
        e0 = jnp.repeat(conv_prev[:, 0, :], seq_len, axis=0)
        e1 = jnp.repeat(conv_prev[:, 1, :], seq_len, axis=0)
        e_spec = pl.BlockSpec((bm, FFN_BF), lambda i, j: (i, j))
        ut_shape = jax.ShapeDtypeStruct((m, D_FF_PAD), F32)
        ut_spec = pl.BlockSpec((bm, FFN_BF), lambda i, j: (i, j))
        whole_u = True
    kern = functools.partial(_ffn_kernel, alpha=alpha, seq_len=seq_len, bm=bm, whole_u=whole_u)
    return pl.pallas_call(
        kern,
        grid=(m // bm, n_ff),
        in_specs=[
            pl.BlockSpec((bm, D_MODEL), lambda i, j: (i, 0)),
            pl.BlockSpec((D_MODEL, FFN_BF), lambda i, j: (0, j)),
            pl.BlockSpec((D_MODEL, FFN_BF), lambda i, j: (0, j)),
            pl.BlockSpec((FFN_BF, D_MODEL), lambda i, j: (j, 0)),
            pl.BlockSpec((CONV_W, FFN_BF), lambda i, j: (0, j)),
            pl.BlockSpec((1, FFN_BF), lambda i, j: (0, j)),
            pl.BlockSpec((1, D_MODEL), lambda i, j: (0, 0)),
            pl.BlockSpec((1, D_MODEL), lambda i, j: (0, 0)),
            e_spec,
            e_spec,
        ],
        out_specs=[pl.BlockSpec((bm, D_MODEL), lambda i, j: (i, 0)), ut_spec],
        out_shape=[jax.ShapeDtypeStruct((m, D_MODEL), F32), ut_shape],
        scratch_shapes=[pltpu.VMEM((bm, D_MODEL), BF16),
                        pltpu.VMEM((n_ff, 8, FFN_BF), F32)],
        compiler_params=_cparams(2),
        name="ffn",
    )(h, w_up_u, w_up_g, w_down, conv_w, conv_b, ln_g, ln_b, e0, e1)


def _relayout_w_in_kernel(wt_ref, alo_t_ref, main_ref, alo_ref):
    main_ref[...] = wt_ref[...].T.astype(BF16)

    @pl.when(pl.program_id(0) == 0)
    def _():
        zeros = jnp.zeros((ALO_PAD - GLA_GATE_RANK, alo_t_ref.shape[1]), F32)
        alo_ref[...] = jnp.concatenate([alo_t_ref[...], zeros], axis=0).T.astype(BF16)


def _relayout_w_in(w_in_t):
    a1 = Z_MG + GLA_GATE_RANK
    n_att = 3 * ATT_WIDTH
    n_head, n_gate = Z_MG // INPROJ_BN, (Z_QA - Z_MG) // INPROJ_BN

    sub = 8
    assert a1 % sub == 0 and n_att % sub == 0 and INPROJ_BN % sub == 0

    def src_row(j):
        gates = (a1 + n_att) // sub + (j - n_head) * (INPROJ_BN // sub)
        att = a1 // sub + (j - n_head - n_gate) * (INPROJ_BN // sub)
        head = j * (INPROJ_BN // sub)
        return jnp.where(j < n_head, head, jnp.where(j < n_head + n_gate, gates, att)) * sub

    return pl.pallas_call(
        _relayout_w_in_kernel,
        grid=(Z_WIDTH // INPROJ_BN,),
        in_specs=[pl.BlockSpec((pl.Element(INPROJ_BN), pl.Element(D_MODEL)), lambda j: (src_row(j), 0)),
                  pl.BlockSpec((pl.Element(GLA_GATE_RANK), pl.Element(D_MODEL)), lambda j: (Z_MG, 0))],
        out_specs=[pl.BlockSpec((D_MODEL, INPROJ_BN), lambda j: (0, j)),
                   pl.BlockSpec((D_MODEL, ALO_PAD), lambda j: (0, 0))],
        out_shape=[jax.ShapeDtypeStruct((D_MODEL, Z_WIDTH), BF16),
                   jax.ShapeDtypeStruct((D_MODEL, ALO_PAD), BF16)],
        compiler_params=_cparams(1),
        name="relayout_w_in",
    )(w_in_t, w_in_t)


def _split_w_up_kernel(w_ref, u_ref, g_ref):
    pad = jnp.zeros((w_ref.shape[0], D_FF_PAD - D_FF), BF16)
    u_ref[:, 0:D_FF] = w_ref[:, 0:D_FF].astype(BF16)
    u_ref[:, D_FF:D_FF_PAD] = pad
    g_ref[:, 0:D_FF] = w_ref[:, D_FF:2 * D_FF].astype(BF16)
    g_ref[:, D_FF:D_FF_PAD] = pad


def _split_w_up(w_ffn_up, layer):
    k = w_ffn_up.shape[1]
    half = pl.BlockSpec((PREP_ROWS, D_FF_PAD), lambda i: (i, 0))
    return pl.pallas_call(
        _split_w_up_kernel,
        grid=(k // PREP_ROWS,),
        in_specs=[pl.BlockSpec((None, PREP_ROWS, 2 * D_FF), lambda i: (layer, i, 0))],
        out_specs=[half, half],
        out_shape=[jax.ShapeDtypeStruct((k, D_FF_PAD), BF16)] * 2,
        compiler_params=_cparams(1),
        name="split_w_up",
    )(w_ffn_up)


def _prepare_params(layer, w_in_all, w_ffn_up_all, gla_gate_up, gla_gate_b, gla_norm_g, w_br_gla, w_br_att,
                    w_out, ln1_g, ln1_b, ffn_conv_w, ffn_conv_b, w_ffn_down, ln2_g, ln2_b):
    w_main, w_alo = _relayout_w_in(w_in_all[layer].T)
    w_up_u, w_up_g = _split_w_up(w_ffn_up_all, layer)
    gup = jnp.pad(gla_gate_up, ((0, ALO_PAD - GLA_GATE_RANK), (0, 0))).astype(BF16)
    ffpad = D_FF_PAD - D_FF
    return dict(
        w_main=w_main, w_alo=w_alo, gup=gup,
        gate_b=gla_gate_b.reshape(1, GLA_QK), norm_g=gla_norm_g.reshape(1, GLA_V),
        w_br_gla=w_br_gla.astype(BF16), w_br_att=w_br_att.astype(BF16), w_out=w_out.astype(BF16),
        ln1_g=ln1_g.reshape(1, D_MODEL), ln1_b=ln1_b.reshape(1, D_MODEL),
        w_up_u=w_up_u, w_up_g=w_up_g,
        w_down=jnp.pad(w_ffn_down, ((0, ffpad), (0, 0))).astype(BF16),
        conv_w=jnp.pad(ffn_conv_w, ((0, 0), (0, ffpad))),
        conv_b=jnp.pad(ffn_conv_b.reshape(1, D_FF), ((0, 0), (0, ffpad))),
        ln2_g=ln2_g.reshape(1, D_MODEL), ln2_b=ln2_b.reshape(1, D_MODEL),
    )


def _layer(x, past_k, past_v, s0, conv_prev, p, table, merge_b, *, alpha):
    n_seq, seq_len, _ = x.shape
    m = n_seq * seq_len
    x2d = x.reshape(m, D_MODEL)
    z, alo, k_new, v_new = _inproj(x2d, p["w_main"], p["w_alo"], seq_len=seq_len)
    o_gla, s_fin = _gla(z, alo, p["gup"], p["gate_b"], s0, n_seq=n_seq, seq_len=seq_len)
    keep = min(BAND_PAST, seq_len)
    k_new = k_new.reshape(n_seq, keep, ATT_HEADS, ATT_DH)
    v_new = v_new.reshape(n_seq, keep, ATT_HEADS, ATT_DH)
    if past_k is None:
        bias = _rel_bias(table, n_q=ATT_GROUP, n_k=ATT_WINDOW, off=BAND_PAST, band=True)
        o_att = _band_attention(z, bias, n_seq=n_seq, seq_len=seq_len)
    else:
        n_past = past_k.shape[1]
        n_k = ((n_past + seq_len + V7X_LANES - 1) // V7X_LANES) * V7X_LANES
        bias = _rel_bias(table, n_q=seq_len, n_k=n_k, off=n_past, band=False)
        o_att = _band_attention_step(z, past_k, past_v, bias, n_seq=n_seq, n_new=seq_len)
    h = _merge(o_gla, o_att, z, x2d, p["norm_g"], p["w_br_gla"], p["w_br_att"], p["w_out"], merge_b,
               p["ln1_g"], p["ln1_b"], alpha=alpha)
    prev = jnp.pad(conv_prev, ((0, 0), (0, 0), (0, D_FF_PAD - D_FF)))
    y, ut = _ffn(h, p["w_up_u"], p["w_up_g"], p["w_down"], p["conv_w"], p["conv_b"],
                 p["ln2_g"], p["ln2_b"], prev, alpha=alpha, n_seq=n_seq, seq_len=seq_len)
    if seq_len >= FFN_BM:
        tiles_per_seq = seq_len // FFN_BM
        conv_new = ut.reshape(n_seq, tiles_per_seq, 8, D_FF_PAD)[:, -1, 8 - (CONV_W - 1):, :D_FF]
    else:
        conv_new = ut.reshape(n_seq, seq_len, D_FF_PAD)[:, seq_len - (CONV_W - 1):, :D_FF]
    return (y.reshape(n_seq, seq_len, D_MODEL), k_new, v_new, s_fin, conv_new)


def kernel(x_prompt, x_sample, cache_att_k, cache_att_v, state_gla, state_ffn_conv, w_in, gla_gate_up,
           gla_gate_b, gla_norm_g, att_rel_bias, merge_b, w_br_gla, w_br_att, w_out, ln1_g, ln1_b,
           w_ffn_up, ffn_conv_w, ffn_conv_b, w_ffn_down, ln2_g, ln2_b):
    depth = w_in.shape[0]
    alpha = (2.0 * depth) ** 0.25
    xp, xs = x_prompt, x_sample
    n_p = xp.shape[0]
    outs_p, outs_s = [], []
    for l in range(depth):
        p = _prepare_params(l, w_in, w_ffn_up, gla_gate_up[l], gla_gate_b[l], gla_norm_g[l], w_br_gla[l],
                            w_br_att[l], w_out[l], ln1_g[l], ln1_b[l], ffn_conv_w[l],
                            ffn_conv_b[l], w_ffn_down[l], ln2_g[l], ln2_b[l])
        s0_p = jnp.zeros((n_p, GLA_HEADS, GLA_DK, GLA_DV), F32)
        conv0_p = jnp.zeros((n_p, CONV_W - 1, D_FF), F32)
        xp, *rest_p = _layer(xp, None, None, s0_p, conv0_p, p, att_rel_bias[l], merge_b[l], alpha=alpha)
        outs_p.append(rest_p)
        xs, *rest_s = _layer(xs, cache_att_k[l], cache_att_v[l], state_gla[l], state_ffn_conv[l], p,
                             att_rel_bias[l], merge_b[l], alpha=alpha)
        outs_s.append(rest_s)
    stack = lambda outs, idx: jnp.stack([o[idx] for o in outs])
    return (xp, xs,
            stack(outs_p, 0), stack(outs_p, 1), stack(outs_p, 2), stack(outs_p, 3),
            stack(outs_s, 0), stack(outs_s, 1), stack(outs_s, 2), stack(outs_s, 3))
```
